```python
import math
import jax, jax.numpy as jnp
from jax import lax
import numpy as np

D_MODEL = 2048
BATCH = 32
SEQ = 256
DEPTH = 4
DEC_BATCH = 4
DEC_SEQ = 2048
PAST_LEN = 512

GRID_W = 64
N_MOD = 6
EPS = 1e-6
A_WIDTH = 1024
A_HEADS = 8
A_DH = A_WIDTH // A_HEADS // 2
A_DV = 2 * A_DH
ROPE_FREQ = A_DH // 4
ROPE_THETA = 10000.0
Q_BLOCK = 128
P_WIDTH = 512
P_GROUPS = 4
P_GC = P_WIDTH // P_GROUPS
P_WINDOWS = (2, 4, 8, 16)
C_WIDTH = 512
C_KSIZE = 3
N_BRANCH = 3
N_GROUPS = 4
E_PER_GROUP = 4
N_EXPERTS = N_GROUPS * E_PER_GROUP
TOP_K = 2
D_EXPERT = 512
IN_SIZES = (A_WIDTH, A_WIDTH, A_WIDTH, P_WIDTH, C_WIDTH, C_WIDTH, C_WIDTH, N_BRANCH * D_MODEL)
IN_COLS = sum(IN_SIZES)

kernel_name = 'hybrid_diffattn_pool_conv_hmoe_step'


def rms_norm(x, g):
    xf = x.astype(jnp.float32)
    y = xf * lax.rsqrt(jnp.mean(xf * xf, axis=-1, keepdims=True) + EPS)
    return (y * g.astype(jnp.float32)).astype(x.dtype)


def split_cols(y, sizes):
    outs, start = [], 0
    for s in sizes:
        outs.append(y[..., start:start + s])
        start += s
    return outs


def axial_rope_tables(n_tokens, dtype):
    rows = n_tokens // GRID_W
    f32 = jnp.float32
    pos_r = jnp.repeat(jnp.arange(rows, dtype=f32), GRID_W)
    pos_c = jnp.tile(jnp.arange(GRID_W, dtype=f32), rows)
    inv_freq = jnp.power(ROPE_THETA, -jnp.arange(ROPE_FREQ, dtype=f32) / ROPE_FREQ)
    ang = jnp.stack([pos_r[:, None] * inv_freq, pos_c[:, None] * inv_freq], axis=1)
    return jnp.cos(ang).astype(dtype), jnp.sin(ang).astype(dtype)


def apply_axial_rope(x, cos, sin):
    shp = x.shape
    xr = x.reshape(shp[:-1] + (2, 2, ROPE_FREQ))
    c = cos[:, None, None]
    s = sin[:, None, None]
    x1, x2 = xr[..., 0, :], xr[..., 1, :]
    out = jnp.stack([x1 * c - x2 * s, x1 * s + x2 * c], axis=-2)
    return out.reshape(shp)


def diff_attention(q, k, v, lam):
    b, tq = q.shape[:2]
    nb = tq // Q_BLOCK
    qb = q.reshape(b, nb, Q_BLOCK, A_HEADS, 2, A_DH).swapaxes(0, 1)
    scale = A_DH ** -0.5

    def block(qi):
        s = jnp.einsum('bqhcd,bkhcd->bhcqk', qi, k).astype(jnp.float32) * scale
        p = jax.nn.softmax(s, axis=-1)
        w = (p[:, :, 0] - lam * p[:, :, 1]).astype(v.dtype)
        return jnp.einsum('bhqk,bkhe->bqhe', w, v)

    out = lax.map(block, qb)
    return out.swapaxes(0, 1).reshape(b, tq, A_HEADS, A_DV)


def multiscale_pool(u, w_pool, pool_scale):
    b, t, _ = u.shape
    f32 = jnp.float32
    ug = u.reshape(b, t, P_GROUPS, P_GC)
    cs = jnp.cumsum(ug.astype(f32), axis=1)
    cs = jnp.concatenate([jnp.zeros_like(cs[:, :1]), cs], axis=1)
    pos = jnp.arange(t)[:, None]
    half = jnp.array([w // 2 for w in P_WINDOWS], dtype=jnp.int32)[None, :]
    lo = jnp.clip(pos - half, 0, t)
    hi = jnp.clip(pos + half, 0, t)
    full = (b, t, P_GROUPS, P_GC)
    s_hi = jnp.take_along_axis(cs, jnp.broadcast_to(hi[None, :, :, None], full), axis=1)
    s_lo = jnp.take_along_axis(cs, jnp.broadcast_to(lo[None, :, :, None], full), axis=1)
    mean = (s_hi - s_lo) / (hi - lo).astype(f32)[None, :, :, None]
    y = (mean - ug.astype(f32)).astype(u.dtype)
    y = jnp.einsum('btgc,gce->btge', y, w_pool).reshape(b, t, P_WIDTH)
    return y * pool_scale


def short_conv(u, gate_b, gate_c, conv_w, conv_b):
    t = u.shape[1]
    z = gate_c * u
    zp = jnp.pad(z, ((0, 0), (1, 1), (0, 0)))
    conv = zp[:, 0:t] * conv_w[0] + zp[:, 1:t + 1] * conv_w[1] + zp[:, 2:t + 2] * conv_w[2] + conv_b
    return gate_b * conv


def hier_moe(h, w_rg, b_rg, w_re, b_re, w_gate, w_up, w_down):
    b, t, d = h.shape
    f32 = jnp.float32
    xt = h.reshape(b * t, d)
    g_prob = jax.nn.softmax((xt @ w_rg).astype(f32) + b_rg.astype(f32), axis=-1)
    g_sel = jnp.argmax(g_prob, axis=-1)
    g_w = jnp.max(g_prob, axis=-1, keepdims=True)
    e_logit = ((xt @ w_re).astype(f32) + b_re.astype(f32)).reshape(-1, N_GROUPS, E_PER_GROUP)
    e_logit = jnp.einsum('nge,ng->ne', e_logit, jax.nn.one_hot(g_sel, N_GROUPS, dtype=f32))
    top_w, top_i = lax.top_k(jax.nn.softmax(e_logit, axis=-1), TOP_K)
    top_w = g_w * top_w / jnp.sum(top_w, axis=-1, keepdims=True)
    expert_id = g_sel[:, None] * E_PER_GROUP + top_i
    comb = jnp.einsum('nke,nk->ne', jax.nn.one_hot(expert_id, N_EXPERTS, dtype=f32), top_w)
    act = jax.nn.silu(jnp.einsum('nd,edf->nef', xt, w_gate)) * jnp.einsum('nd,edf->nef', xt, w_up)
    y = jnp.einsum('nef,efd->nd', act * comb.astype(act.dtype)[..., None], w_down)
    return y.reshape(b, t, d)


def trunk_layer(x, mod, lam_init, rope, ctx_k, ctx_v,
                norm1_g, norm2_g, w_in, q_norm_g, k_norm_g,
                lam_q1, lam_k1, lam_q2, lam_k2, subln_g, w_pool, pool_scale,
                conv_w, conv_b, w_br_a, w_br_p, w_br_c, w_out,
                w_route_group, b_route_group, w_route_expert, b_route_expert,
                w_exp_gate, w_exp_up, w_exp_down):
    b, t, _ = x.shape
    shift1, scale1, gate1, shift2, scale2, gate2 = (mod[:, i][:, None, :] for i in range(N_MOD))
    h = rms_norm(x, norm1_g) * (1.0 + scale1) + shift1
    q, k, v, u_pool, u_conv, gate_b, gate_c, g_lin = split_cols(h @ w_in, IN_SIZES)
    q = rms_norm(q.reshape(b, t, A_HEADS, 2, A_DH), q_norm_g)
    k = rms_norm(k.reshape(b, t, A_HEADS, 2, A_DH), k_norm_g)
    v = v.reshape(b, t, A_HEADS, A_DV)
    if rope is None:
        k_all, v_all = k, v
    else:
        cos, sin = rope
        q = apply_axial_rope(q, cos, sin)
        k_all = jnp.concatenate([apply_axial_rope(k, cos, sin), ctx_k.astype(k.dtype)], axis=1)
        v_all = jnp.concatenate([v, ctx_v.astype(v.dtype)], axis=1)
    f32 = jnp.float32
    lam = (jnp.exp(jnp.sum(lam_q1.astype(f32) * lam_k1.astype(f32)))
           - jnp.exp(jnp.sum(lam_q2.astype(f32) * lam_k2.astype(f32))) + lam_init)
    o_a = diff_attention(q, k_all, v_all, lam)
    o_a = (rms_norm(o_a, subln_g) * (1.0 - lam_init)).reshape(b, t, A_WIDTH)
    o_p = multiscale_pool(u_pool, w_pool, pool_scale)
    o_c = short_conv(u_conv, gate_b, gate_c, conv_w, conv_b)
    gates = jax.nn.sigmoid(g_lin.reshape(b, t, N_BRANCH, D_MODEL))
    merged = (gates[:, :, 0] * (o_a @ w_br_a) + gates[:, :, 1] * (o_p @ w_br_p)
              + gates[:, :, 2] * (o_c @ w_br_c))
    x = x + gate1 * (merged @ w_out)
    h2 = rms_norm(x, norm2_g) * (1.0 + scale2) + shift2
    x = x + gate2 * hier_moe(h2, w_route_group, b_route_group, w_route_expert, b_route_expert,
                             w_exp_gate, w_exp_up, w_exp_down)
    return x, k, v


def setup_inputs(seed: int = 0) -> dict:
    key = jax.random.key(seed)
    ks = iter(jax.random.split(key, 48))
    L, D = DEPTH, D_MODEL

    def nrm(shape, s):
        return jax.random.normal(next(ks), shape, jnp.float32) * s

    return {
        'x_prompt': nrm((BATCH, SEQ, D), 1.0),
        'x_sample': nrm((DEC_BATCH, DEC_SEQ, D), 1.0),
        'cache_k': nrm((DEC_BATCH, L, PAST_LEN, A_HEADS, 2, A_DH), 1.0),
        'cache_v': nrm((DEC_BATCH, L, PAST_LEN, A_HEADS, A_DV), 1.0),
        'c': nrm((DEC_BATCH, D), 1.0),
        'c_ctx': nrm((D,), 1.0),
        'w_ada': nrm((L, D, N_MOD * D), 0.5 * D ** -0.5),
        'b_ada': nrm((L, N_MOD * D), 0.02),
        'norm1_g': 1.0 + nrm((L, D), 0.1),
        'norm2_g': 1.0 + nrm((L, D), 0.1),
        'w_in': nrm((L, D, IN_COLS), D ** -0.5),
        'q_norm_g': 1.0 + nrm((L, A_DH), 0.1),
        'k_norm_g': 1.0 + nrm((L, A_DH), 0.1),
        'lam_q1': nrm((L, A_DH), 0.1),
        'lam_k1': nrm((L, A_DH), 0.1),
        'lam_q2': nrm((L, A_DH), 0.1),
        'lam_k2': nrm((L, A_DH), 0.1),
        'subln_g': 1.0 + nrm((L, A_DV), 0.1),
        'w_pool': nrm((L, P_GROUPS, P_GC, P_GC), P_GC ** -0.5),
        'pool_scale': 0.5 + nrm((L, P_WIDTH), 0.1),
        'conv_w': nrm((L, C_KSIZE, C_WIDTH), C_KSIZE ** -0.5),
        'conv_b': nrm((L, C_WIDTH), 0.02),
        'w_br_a': nrm((L, A_WIDTH, D), A_WIDTH ** -0.5),
        'w_br_p': nrm((L, P_WIDTH, D), P_WIDTH ** -0.5),
        'w_br_c': nrm((L, C_WIDTH, D), C_WIDTH ** -0.5),
        'w_out': nrm((L, D, D), D ** -0.5),
        'w_route_group': nrm((L, D, N_GROUPS), D ** -0.5),
        'b_route_group': nrm((L, N_GROUPS), 0.01),
        'w_route_expert': nrm((L, D, N_EXPERTS), D ** -0.5),
        'b_route_expert': nrm((L, N_EXPERTS), 0.01),
        'w_exp_gate': nrm((L, N_EXPERTS, D, D_EXPERT), D ** -0.5),
        'w_exp_up': nrm((L, N_EXPERTS, D, D_EXPERT), D ** -0.5),
        'w_exp_down': nrm((L, N_EXPERTS, D_EXPERT, D), D_EXPERT ** -0.5),
    }


def reference(x_prompt, x_sample, cache_k, cache_v, c, c_ctx,
              w_ada, b_ada, norm1_g, norm2_g, w_in, q_norm_g, k_norm_g,
              lam_q1, lam_k1, lam_q2, lam_k2, subln_g, w_pool, pool_scale,
              conv_w, conv_b, w_br_a, w_br_p, w_br_c, w_out,
              w_route_group, b_route_group, w_route_expert, b_route_expert,
              w_exp_gate, w_exp_up, w_exp_down):
    rope = axial_rope_tables(x_sample.shape[1], x_sample.dtype)
    ctx_cond = jax.nn.silu(c_ctx)[None]
    lat_cond = jax.nn.silu(c)
    y_p, y_s = x_prompt, x_sample
    new_k, new_v = [], []
    for l in range(DEPTH):
        lam_init = 0.8 - 0.6 * math.exp(-0.3 * l)
        lp = [a[l] for a in (norm1_g, norm2_g, w_in, q_norm_g, k_norm_g,
                             lam_q1, lam_k1, lam_q2, lam_k2, subln_g, w_pool, pool_scale,
                             conv_w, conv_b, w_br_a, w_br_p, w_br_c, w_out,
                             w_route_group, b_route_group, w_route_expert, b_route_expert,
                             w_exp_gate, w_exp_up, w_exp_down)]
        mod_ctx = (ctx_cond @ w_ada[l] + b_ada[l]).reshape(1, N_MOD, D_MODEL)
        mod_lat = (lat_cond @ w_ada[l] + b_ada[l]).reshape(-1, N_MOD, D_MODEL)
        y_p, k_ctx, v_ctx = trunk_layer(y_p, mod_ctx, lam_init, None, None, None, *lp)
        new_k.append(k_ctx)
        new_v.append(v_ctx)
        y_s, _, _ = trunk_layer(y_s, mod_lat, lam_init, rope, cache_k[:, l], cache_v[:, l], *lp)
    new_cache_k = jnp.stack(new_k, axis=1)
    new_cache_v = jnp.stack(new_v, axis=1)
    return (y_p, y_s, new_cache_k, new_cache_v)
```

```python
import functools

import numpy as np
import jax
import jax.numpy as jnp
from jax import lax
from jax.experimental import pallas as pl
from jax.experimental.pallas import tpu as pltpu

F32 = jnp.float32
BF16 = jnp.bfloat16
I32 = jnp.int32

D = 2048
N_MOD = 6
EPS = 1e-6
GRID_W = 64
A_WIDTH = 1024
A_HEADS = 8
A_DH = 64
A_DV = 128
ROPE_FREQ = 16
ROPE_THETA = 10000.0
P_WIDTH = 512
P_GROUPS = 4
P_GC = 128
P_WINDOWS = (2, 4, 8, 16)
C_WIDTH = 512
N_GROUPS = 4
E_PER_GROUP = 4
N_EXPERTS = 16
D_EXPERT = 512

LANES = 128
PROJ_COLS = 11264
COL_BLK = 1024
J_Q, J_K, J_V, J_PC, J_BC = 6, 7, 8, 9, 10
N_PAIRS = 6
N_CLASSES = N_GROUPS * N_PAIRS
XS_COLS = D + LANES

TM_IN = 512
TQ = 256
TM_MIX = 2048
TM_POST = 256
TM_DISP = 512
TM_MOE = 256
TM_COMB = 256

_PAIRS = [(a, b) for a in range(E_PER_GROUP) for b in range(a + 1, E_PER_GROUP)]
_CLASS_EA = np.array([g * E_PER_GROUP + _PAIRS[p][0] for g in range(N_GROUPS) for p in range(N_PAIRS)], np.int32)
_CLASS_EB = np.array([g * E_PER_GROUP + _PAIRS[p][1] for g in range(N_GROUPS) for p in range(N_PAIRS)], np.int32)


def _cparams(sem, vmem_mb):
    return pltpu.CompilerParams(dimension_semantics=sem, vmem_limit_bytes=vmem_mb * 1024 * 1024)


def _dot(a, b):
    return jnp.dot(a, b, preferred_element_type=F32)


def _dot_nt(a, b):
    return lax.dot_general(a, b, (((1,), (1,)), ((), ())), preferred_element_type=F32)


def _ada_kernel(c_ref, w_ref, b_ref, o_ref):
    c = c_ref[...]
    a = (c * jax.nn.sigmoid(c)).astype(BF16)
    o_ref[0] = _dot(a, w_ref[0].astype(BF16)) + b_ref[0]


def _ada_call(cond8, w_ada, b_ada3):
    L = w_ada.shape[0]
    tn = 1024
    return pl.pallas_call(
        _ada_kernel,
        grid=(L, N_MOD * D // tn),
        in_specs=[
            pl.BlockSpec((8, D), lambda l, j: (0, 0)),
            pl.BlockSpec((1, D, tn), lambda l, j: (l, 0, j)),
            pl.BlockSpec((1, 1, tn), lambda l, j: (l, 0, j)),
        ],
        out_specs=pl.BlockSpec((1, 8, tn), lambda l, j: (l, 0, j)),
        out_shape=jax.ShapeDtypeStruct((L, 8, N_MOD * D), F32),
        compiler_params=_cparams(("arbitrary", "arbitrary"), 40),
        name="ada_mod",
    )(cond8, w_ada, b_ada3)


def _group_rms(x, gain):
    lane = lax.broadcasted_iota(I32, x.shape, 1)
    lo = lane < A_DH
    sq = x * x
    s_lo = jnp.sum(jnp.where(lo, sq, 0.0), axis=-1, keepdims=True)
    s_hi = jnp.sum(jnp.where(lo, 0.0, sq), axis=-1, keepdims=True)
    ms = jnp.where(lo, s_lo, s_hi) * (1.0 / A_DH)
    return x * lax.rsqrt(ms + EPS) * gain


def _rope(x, c, sm, sp):
    up = pltpu.roll(x, LANES - ROPE_FREQ, 1)
    dn = pltpu.roll(x, ROPE_FREQ, 1)
    return x * c + up * sm + dn * sp


def _inproj_kernel(x_ref, mod_ref, g1_ref, w_ref, qg_ref, kg_ref, rc_ref, rsm_ref, rsp_ref,
                   kin_ref, vin_ref, proj_ref, kc_ref, vc_ref, h_ref, *, n_ctx_tiles):
    del kin_ref, vin_ref
    i = pl.program_id(0)
    j = pl.program_id(1)

    @pl.when(j == 0)
    def _():
        x = x_ref[...]
        ms = jnp.mean(x * x, axis=-1, keepdims=True)
        y = x * lax.rsqrt(ms + EPS) * g1_ref[...]
        h = y * (1.0 + mod_ref[0, 1:2, :]) + mod_ref[0, 0:1, :]
        h_ref[...] = h.astype(BF16)

    acc = _dot(h_ref[...], w_ref[0])

    @pl.when(j < J_Q)
    def _():
        proj_ref[...] = jax.nn.sigmoid(acc).astype(BF16)

    @pl.when(j == J_Q)
    def _():
        c, sm, sp = rc_ref[...], rsm_ref[...], rsp_ref[...]
        for h in range(A_HEADS):
            sl = slice(h * LANES, (h + 1) * LANES)
            y = _group_rms(acc[:, sl], qg_ref[...])
            y = _rope(y, c, sm, sp) * (A_DH ** -0.5)
            proj_ref[:, sl] = y.astype(BF16)

    @pl.when(j == J_K)
    def _():
        c, sm, sp = rc_ref[...], rsm_ref[...], rsp_ref[...]
        for h in range(A_HEADS):
            sl = slice(h * LANES, (h + 1) * LANES)
            y = _group_rms(acc[:, sl], kg_ref[...])
            proj_ref[:, sl] = _rope(y, c, sm, sp).astype(BF16)

            @pl.when(i < n_ctx_tiles)
            def _():
                kc_ref[:, 0, :, sl] = y.reshape(kc_ref.shape[0], kc_ref.shape[2], LANES)

    @pl.when(j == J_V)
    def _():
        proj_ref[...] = acc.astype(BF16)

        @pl.when(i < n_ctx_tiles)
        def _():
            vc_ref[:, 0, :, :] = acc.reshape(vc_ref.shape[0], vc_ref.shape[2], A_WIDTH)

    @pl.when(j > J_V)
    def _():
        proj_ref[...] = acc.astype(BF16)


def _inproj_call(l, x_all, mod_l, g1, w_in_r, qg, kg, rope_c, rope_sm, rope_sp, kc, vc, dims):
    NC, NL, S, DS = dims
    M = NC + NL
    tm = TM_IN
    nct = NC // tm
    tpl = DS // tm
    seq_per_tile = tm // S

    def modrow(i):
        return jnp.where(i < nct, 0, 1 + (i - nct) // tpl)

    def roperow(i):
        return jnp.where(i < nct, 0, 1 + (i - nct) % tpl)

    def cache_idx(i, j):
        return (jnp.minimum(i, nct - 1), l, 0, 0)

    B, L = kc.shape[0], kc.shape[1]
    kern = functools.partial(_inproj_kernel, n_ctx_tiles=nct)
    return pl.pallas_call(
        kern,
        grid=(M // tm, PROJ_COLS // COL_BLK),
        in_specs=[
            pl.BlockSpec((tm, D), lambda i, j: (i, 0)),
            pl.BlockSpec((1, N_MOD, D), lambda i, j: (modrow(i), 0, 0)),
            pl.BlockSpec((1, D), lambda i, j: (0, 0)),
            pl.BlockSpec((1, D, COL_BLK), lambda i, j: (l, 0, j)),
            pl.BlockSpec((1, LANES), lambda i, j: (0, 0)),
            pl.BlockSpec((1, LANES), lambda i, j: (0, 0)),
            pl.BlockSpec((tm, LANES), lambda i, j: (roperow(i), 0)),
            pl.BlockSpec((tm, LANES), lambda i, j: (roperow(i), 0)),
            pl.BlockSpec((tm, LANES), lambda i, j: (roperow(i), 0)),
            pl.BlockSpec(memory_space=pl.ANY),
            pl.BlockSpec(memory_space=pl.ANY),
        ],
        out_specs=[
            pl.BlockSpec((tm, COL_BLK), lambda i, j: (i, j)),
            pl.BlockSpec((seq_per_tile, 1, S, A_WIDTH), cache_idx),
            pl.BlockSpec((seq_per_tile, 1, S, A_WIDTH), cache_idx),
        ],
        out_shape=[
            jax.ShapeDtypeStruct((M, PROJ_COLS), BF16),
            jax.ShapeDtypeStruct(kc.shape, F32),
            jax.ShapeDtypeStruct(vc.shape, F32),
        ],
        scratch_shapes=[pltpu.VMEM((tm, D), BF16)],
        input_output_aliases={9: 1, 10: 2},
        compiler_params=_cparams(("arbitrary", "arbitrary"), 48),
        name="in_proj",
    )(x_all, mod_l, g1, w_in_r, qg, kg, rope_c, rope_sm, rope_sp, kc, vc)


def _lambda(lq1, lk1, lq2, lk2, lam_init):
    a = jnp.sum(lq1[...] * lk1[...], axis=-1, keepdims=True)
    b = jnp.sum(lq2[...] * lk2[...], axis=-1, keepdims=True)
    return jnp.exp(a) - jnp.exp(b) + lam_init


def _attn_head(q_h, ks, vs, lam, sg, lam_init):
    lane = lax.broadcasted_iota(I32, q_h.shape, 1)
    lo = lane < A_DH
    zero = jnp.zeros_like(q_h)
    q0 = jnp.where(lo, q_h, zero)
    q1 = jnp.where(lo, zero, q_h)

    def soft(qm):
        ss = [_dot_nt(qm, k) for k in ks]
        m = functools.reduce(jnp.maximum, [jnp.max(s, axis=-1, keepdims=True) for s in ss])
        es = [jnp.exp(s - m) for s in ss]
        den = functools.reduce(lambda a, b: a + b, [jnp.sum(e, axis=-1, keepdims=True) for e in es])
        return es, den

    e0, l0 = soft(q0)
    e1, l1 = soft(q1)
    r0 = 1.0 / l0
    r1 = lam / l1
    o = None
    for a, b, v in zip(e0, e1, vs):
        part = _dot((a * r0 - b * r1).astype(BF16), v)
        o = part if o is None else o + part
    ms = jnp.mean(o * o, axis=-1, keepdims=True)
    return (o * lax.rsqrt(ms + EPS) * sg) * (1.0 - lam_init)


def _attn_ctx_kernel(q_ref, k_ref, v_ref, lq1, lk1, lq2, lk2, sg_ref, o_ref, *, lam_init):
    lam = _lambda(lq1, lk1, lq2, lk2, lam_init)
    for h in range(A_HEADS):
        sl = slice(h * LANES, (h + 1) * LANES)
        o = _attn_head(q_ref[:, sl], [k_ref[:, sl]], [v_ref[:, sl]], lam, sg_ref[...], lam_init)
        o_ref[:, sl] = o.astype(BF16)


def _attn_lat_kernel(q_ref, k_ref, v_ref, ck_ref, cv_ref, lq1, lk1, lq2, lk2, sg_ref, oin_ref, o_ref,
                     *, lam_init):
    del oin_ref
    lam = _lambda(lq1, lk1, lq2, lk2, lam_init)
    for h in range(A_HEADS):
        sl = slice(h * LANES, (h + 1) * LANES)
        ks = [k_ref[:, sl], ck_ref[0, 0, :, sl].astype(BF16)]
        vs = [v_ref[:, sl], cv_ref[0, 0, :, sl].astype(BF16)]
        o = _attn_head(q_ref[:, sl], ks, vs, lam, sg_ref[...], lam_init)
        o_ref[:, sl] = o.astype(BF16)


def _attention_call(l, proj, cache_k4, cache_v4, lams, sg, lam_init, dims):
    NC, NL, S, DS = dims
    M = NC + NL
    B = NC // S
    DB = NL // DS
    lam_specs1 = [pl.BlockSpec((1, A_DH), lambda b: (0, 0))] * 4
    oa = pl.pallas_call(
        functools.partial(_attn_ctx_kernel, lam_init=lam_init),
        grid=(B,),
        in_specs=[
            pl.BlockSpec((S, A_WIDTH), lambda b: (b, J_Q)),
            pl.BlockSpec((S, A_WIDTH), lambda b: (b, J_K)),
            pl.BlockSpec((S, A_WIDTH), lambda b: (b, J_V)),
            *lam_specs1,
            pl.BlockSpec((1, LANES), lambda b: (0, 0)),
        ],
        out_specs=pl.BlockSpec((S, A_WIDTH), lambda b: (b, 0)),
        out_shape=jax.ShapeDtypeStruct((M, A_WIDTH), BF16),
        compiler_params=_cparams(("arbitrary",), 32),
        name="attn_ctx",
    )(proj, proj, proj, *lams, sg)

    nq = DS // TQ
    q0 = NC // TQ
    k0 = NC // DS
    P = cache_k4.shape[2]
    lam_specs2 = [pl.BlockSpec((1, A_DH), lambda b, qi: (0, 0))] * 4
    oa = pl.pallas_call(
        functools.partial(_attn_lat_kernel, lam_init=lam_init),
        grid=(DB, nq),
        in_specs=[
            pl.BlockSpec((TQ, A_WIDTH), lambda b, qi: (q0 + b * nq + qi, J_Q)),
            pl.BlockSpec((DS, A_WIDTH), lambda b, qi: (k0 + b, J_K)),
            pl.BlockSpec((DS, A_WIDTH), lambda b, qi: (k0 + b, J_V)),
            pl.BlockSpec((1, 1, P, A_WIDTH), lambda b, qi: (b, l, 0, 0)),
            pl.BlockSpec((1, 1, P, A_WIDTH), lambda b, qi: (b, l, 0, 0)),
            *lam_specs2,
            pl.BlockSpec((1, LANES), lambda b, qi: (0, 0)),
            pl.BlockSpec(memory_space=pl.ANY),
        ],
        out_specs=pl.BlockSpec((TQ, A_WIDTH), lambda b, qi: (q0 + b * nq + qi, 0)),
        out_shape=jax.ShapeDtypeStruct((M, A_WIDTH), BF16),
        input_output_aliases={10: 0},
        compiler_params=_cparams(("arbitrary", "arbitrary"), 48),
        name="attn_lat",
    )(proj, proj, proj, cache_k4, cache_v4, *lams, sg, oa)
    return oa


def _mix_kernel(pc_ref, bc_ref, wp_ref, ps_ref, cw_ref, cb_ref, o_ref, *, n_ctx_tiles, S, DS):
    i = pl.program_id(0)
    tm = pc_ref.shape[0]
    seqlen = jnp.where(i < n_ctx_tiles, S, DS)
    rows = lax.broadcasted_iota(I32, (tm, 1), 0)
    pos = rows & (seqlen - 1)

    for g, win in enumerate(P_WINDOWS):
        half = win // 2
        sl = slice(g * P_GC, (g + 1) * P_GC)
        u = pc_ref[:, sl].astype(F32)
        acc = jnp.zeros_like(u)
        for s in range(-half, half):
            sh = u if s == 0 else pltpu.roll(u, (-s) % tm, 0)
            ok = jnp.logical_and(pos + s >= 0, pos + s < seqlen)
            acc = acc + jnp.where(ok, sh, 0.0)
        cnt = (jnp.minimum(pos + half, seqlen) - jnp.maximum(pos - half, 0)).astype(F32)
        y = (acc / cnt - u).astype(BF16)
        o_ref[:, sl] = (_dot(y, wp_ref[g]) * ps_ref[:, sl]).astype(BF16)

    u = pc_ref[:, P_WIDTH:].astype(F32)
    gate_b = bc_ref[:, :C_WIDTH].astype(F32)
    gate_c = bc_ref[:, C_WIDTH:].astype(F32)
    z = gate_c * u
    zm = jnp.where(pos >= 1, pltpu.roll(z, 1, 0), 0.0)
    zp = jnp.where(pos + 1 < seqlen, pltpu.roll(z, tm - 1, 0), 0.0)
    conv = zm * cw_ref[0:1, :] + z * cw_ref[1:2, :] + zp * cw_ref[2:3, :] + cb_ref[...]
    o_ref[:, P_WIDTH:] = (gate_b * conv).astype(BF16)


def _mix_call(proj, w_pool_l, pool_scale_l, conv_w_l, conv_b_l, dims):
    NC, NL, S, DS = dims
    M = NC + NL
    tm = TM_MIX
    assert tm % S == 0 and tm == DS and NC % tm == 0
    kern = functools.partial(_mix_kernel, n_ctx_tiles=NC // tm, S=S, DS=DS)
    return pl.pallas_call(
        kern,
        grid=(M // tm,),
        in_specs=[
            pl.BlockSpec((tm, COL_BLK), lambda i: (i, J_PC)),
            pl.BlockSpec((tm, COL_BLK), lambda i: (i, J_BC)),
            pl.BlockSpec((P_GROUPS, P_GC, P_GC), lambda i: (0, 0, 0)),
            pl.BlockSpec((1, P_WIDTH), lambda i: (0, 0)),
            pl.BlockSpec((3, C_WIDTH), lambda i: (0, 0)),
            pl.BlockSpec((1, C_WIDTH), lambda i: (0, 0)),
        ],
        out_specs=pl.BlockSpec((tm, P_WIDTH + C_WIDTH), lambda i: (i, 0)),
        out_shape=jax.ShapeDtypeStruct((M, P_WIDTH + C_WIDTH), BF16),
        compiler_params=_cparams(("arbitrary",), 48),
        name="mixers",
    )(proj, proj, w_pool_l, pool_scale_l, conv_w_l, conv_b_l)


def _route(logits):
    lane = lax.broadcasted_iota(I32, logits.shape, 1)
    lane_f = lane.astype(F32)
    neg = -jnp.inf

    def first_argmax(v, vmax):
        return jnp.min(jnp.where(v == vmax, lane_f, float(LANES)), axis=-1, keepdims=True).astype(I32)

    gl = jnp.where(lane < N_GROUPS, logits, neg)
    gmax = jnp.max(gl, axis=-1, keepdims=True)
    gsel = first_argmax(gl, gmax)
    g_w = 1.0 / jnp.sum(jnp.exp(gl - gmax), axis=-1, keepdims=True)

    first = N_GROUPS + gsel * E_PER_GROUP
    in_grp = jnp.logical_and(lane >= first, lane < first + E_PER_GROUP)
    el = jnp.where(in_grp, logits, neg)
    t1 = jnp.max(el, axis=-1, keepdims=True)
    i1 = first_argmax(el, t1)
    el2 = jnp.where(lane == i1, neg, el)
    t2 = jnp.max(el2, axis=-1, keepdims=True)
    i2 = first_argmax(el2, t2)
    a = jnp.exp(t2 - t1)
    w1 = g_w / (1.0 + a)
    w2 = g_w * a / (1.0 + a)

    swap = i2 < i1
    llo = jnp.where(swap, i2, i1) - first
    lhi = jnp.where(swap, i1, i2) - first
    w_lo = jnp.where(swap, w2, w1)
    w_hi = jnp.where(swap, w1, w2)
    pair = ((llo * (7 - llo)) >> 1) + (lhi - llo - 1)
    cls = gsel * N_PAIRS + pair
    wvec = jnp.where(lane == 0, w_lo, jnp.where(lane == 1, w_hi, 0.0))
    return cls, wvec


def _post_kernel(x_ref, oa_ref, opc_ref, gates_ref, mod_ref, g2_ref, wa_ref, wp_ref, wc_ref, wo_ref,
                 wr_ref, br_ref, x1_ref, h2w_ref, cls_ref):
    a = _dot(oa_ref[...], wa_ref[0])
    merged = gates_ref[:, 0:D].astype(F32) * a
    p = _dot(opc_ref[:, :P_WIDTH], wp_ref[0])
    merged = merged + gates_ref[:, D:2 * D].astype(F32) * p
    c = _dot(opc_ref[:, P_WIDTH:], wc_ref[0])
    merged = merged + gates_ref[:, 2 * D:3 * D].astype(F32) * c
    y = _dot(merged.astype(BF16), wo_ref[0])
    x1 = x_ref[...] + mod_ref[0, 2:3, :] * y
    x1_ref[...] = x1
    ms = jnp.mean(x1 * x1, axis=-1, keepdims=True)
    h2 = x1 * lax.rsqrt(ms + EPS) * g2_ref[...]
    h2 = h2 * (1.0 + mod_ref[0, 4:5, :]) + mod_ref[0, 3:4, :]
    h2w_ref[:, :D] = h2
    logits = _dot(h2.astype(BF16), wr_ref[0]) + br_ref[0]
    cls, wvec = _route(logits)
    h2w_ref[:, D:] = wvec
    cls_ref[...] = jnp.broadcast_to(cls, cls_ref.shape)


def _post_call(l, x_all, oa, opc, proj, mod_l, g2, wa, wp, wc, wo, wr, br, dims):
    NC, NL, S, DS = dims
    M = NC + NL
    tm = TM_POST
    nct = NC // tm
    tpl = DS // tm

    def modrow(i):
        return jnp.where(i < nct, 0, 1 + (i - nct) // tpl)

    const3 = lambda i: (l, 0, 0)
    return pl.pallas_call(
        _post_kernel,
        grid=(M // tm,),
        in_specs=[
            pl.BlockSpec((tm, D), lambda i: (i, 0)),
            pl.BlockSpec((tm, A_WIDTH), lambda i: (i, 0)),
            pl.BlockSpec((tm, P_WIDTH + C_WIDTH), lambda i: (i, 0)),
            pl.BlockSpec((tm, 3 * D), lambda i: (i, 0)),
            pl.BlockSpec((1, N_MOD, D), lambda i: (modrow(i), 0, 0)),
            pl.BlockSpec((1, D), lambda i: (0, 0)),
            pl.BlockSpec((1, A_WIDTH, D), const3, pipeline_mode=pl.Buffered(1)),
            pl.BlockSpec((1, P_WIDTH, D), const3, pipeline_mode=pl.Buffered(1)),
            pl.BlockSpec((1, C_WIDTH, D), const3, pipeline_mode=pl.Buffered(1)),
            pl.BlockSpec((1, D, D), const3, pipeline_mode=pl.Buffered(1)),
            pl.BlockSpec((1, D, LANES), const3, pipeline_mode=pl.Buffered(1)),
            pl.BlockSpec((1, 1, LANES), const3),
        ],
        out_specs=[
            pl.BlockSpec((tm, D), lambda i: (i, 0)),
            pl.BlockSpec((tm, XS_COLS), lambda i: (i, 0)),
            pl.BlockSpec((tm, LANES), lambda i: (i, 0)),
        ],
        out_shape=[
            jax.ShapeDtypeStruct((M, D), F32),
            jax.ShapeDtypeStruct((M, XS_COLS), F32),
            jax.ShapeDtypeStruct((M, LANES), I32),
        ],
        compiler_params=_cparams(("arbitrary",), 52),
        name="post_attn",
    )(x_all, oa, opc, proj, mod_l, g2, wa, wp, wc, wo, wr, br)


def _sort_plan(cls, n_chunks):
    tm = TM_MOE
    onehot = (cls[:, None] == jnp.arange(N_CLASSES, dtype=I32)[None, :]).astype(I32)
    csum = jnp.cumsum(onehot, axis=0)
    rank = jnp.sum(onehot * csum, axis=1) - 1
    counts = csum[-1]
    padded = ((counts + tm - 1) // tm) * tm
    ends = jnp.cumsum(padded)
    starts = ends - padded
    pos = jnp.sum(onehot * starts[None, :], axis=1) + rank
    used = ends[-1] // tm
    chunk = jnp.arange(n_chunks, dtype=I32)
    cidx = jnp.minimum(chunk, used - 1)
    ccls = jnp.sum((ends[None, :] <= (cidx * tm)[:, None]).astype(I32), axis=1)
    ea = jnp.asarray(_CLASS_EA)[ccls]
    eb = jnp.asarray(_CLASS_EB)[ccls]
    valid = (chunk < used).astype(I32)
    return pos.astype(I32), cidx.astype(I32), ea, eb, valid


def _dispatch_kernel(pos_ref, h2w_ref, xs_in_ref, xs_ref, sem):
    del xs_in_ref
    tm = h2w_ref.shape[0]
    base = pl.program_id(0) * tm

    def issue(r, carry):
        dst = pos_ref[base + r]
        pltpu.make_async_copy(h2w_ref.at[pl.ds(r, 1)], xs_ref.at[pl.ds(dst, 1)], sem).start()
        return carry

    lax.fori_loop(0, tm, issue, 0, unroll=8)

    def drain(r, carry):
        pltpu.make_async_copy(h2w_ref.at[pl.ds(0, 1)], xs_ref.at[pl.ds(0, 1)], sem).wait()
        return carry

    lax.fori_loop(0, tm, drain, 0, unroll=8)


def _dispatch_call(pos, h2w, xs):
    M = h2w.shape[0]
    tm = TM_DISP
    return pl.pallas_call(
        _dispatch_kernel,
        grid_spec=pltpu.PrefetchScalarGridSpec(
            num_scalar_prefetch=1,
            grid=(M // tm,),
            in_specs=[
                pl.BlockSpec((tm, XS_COLS), lambda i, pos: (i, 0)),
                pl.BlockSpec(memory_space=pl.ANY),
            ],
            out_specs=pl.BlockSpec(memory_space=pl.ANY),
            scratch_shapes=[pltpu.SemaphoreType.DMA(())],
        ),
        out_shape=jax.ShapeDtypeStruct(xs.shape, F32),
        input_output_aliases={2: 0},
        compiler_params=_cparams(("arbitrary",), 32),
        name="dispatch",
    )(pos, h2w, xs)


def _moe_kernel(cidx_ref, ea_ref, eb_ref, valid_ref, xs_ref, wga, wua, wda, wgb, wub, wdb, ys_ref):
    del cidx_ref, ea_ref, eb_ref
    c = pl.program_id(0)

    @pl.when(valid_ref[c] == 1)
    def _():
        x = xs_ref[:, :D].astype(BF16)
        wts = xs_ref[:, D:]

        def expert(wg, wu, wd, w):
            g = _dot(x, wg[0, 0])
            u = _dot(x, wu[0, 0])
            act = (g * jax.nn.sigmoid(g)) * u
            return _dot((act * w).astype(BF16), wd[0, 0])

        ys_ref[...] = expert(wga, wua, wda, wts[:, 0:1]) + expert(wgb, wub, wdb, wts[:, 1:2])


def _moe_call(l, cidx, ea, eb, valid, xs, wg, wu, wd):
    n_chunks = cidx.shape[0]
    tm = TM_MOE
    up_a = pl.BlockSpec((1, 1, D, D_EXPERT), lambda c, ci, a, b, v: (l, a[c], 0, 0))
    up_b = pl.BlockSpec((1, 1, D, D_EXPERT), lambda c, ci, a, b, v: (l, b[c], 0, 0))
    dn_a = pl.BlockSpec((1, 1, D_EXPERT, D), lambda c, ci, a, b, v: (l, a[c], 0, 0))
    dn_b = pl.BlockSpec((1, 1, D_EXPERT, D), lambda c, ci, a, b, v: (l, b[c], 0, 0))
    return pl.pallas_call(
        _moe_kernel,
        grid_spec=pltpu.PrefetchScalarGridSpec(
            num_scalar_prefetch=4,
            grid=(n_chunks,),
            in_specs=[
                pl.BlockSpec((tm, XS_COLS), lambda c, ci, a, b, v: (ci[c], 0)),
                up_a, up_a, dn_a, up_b, up_b, dn_b,
            ],
            out_specs=pl.BlockSpec((tm, D), lambda c, ci, a, b, v: (ci[c], 0)),
        ),
        out_shape=jax.ShapeDtypeStruct((n_chunks * tm, D), F32),
        compiler_params=_cparams(("arbitrary",), 48),
        name="moe_experts",
    )(cidx, ea, eb, valid, xs, wg, wu, wd, wg, wu, wd)


def _combine_kernel(pos_ref, x1_ref, mod_ref, ys_ref, o_ref, ybuf, sem):
    tm = x1_ref.shape[0]
    i = pl.program_id(0)
    n = pl.num_programs(0)

    def issue(step, slot):
        base = step * tm

        def body(r, carry):
            src = pos_ref[base + r]
            pltpu.make_async_copy(ys_ref.at[pl.ds(src, 1)], ybuf.at[slot, pl.ds(r, 1)], sem.at[slot]).start()
            return carry

        lax.fori_loop(0, tm, body, 0, unroll=8)

    @pl.when(i == 0)
    def _():
        issue(0, 0)

    @pl.when(i + 1 < n)
    def _():
        issue(i + 1, (i + 1) % 2)

    slot = i % 2

    def drain(r, carry):
        pltpu.make_async_copy(ys_ref.at[pl.ds(0, 1)], ybuf.at[slot, pl.ds(0, 1)], sem.at[slot]).wait()
        return carry

    lax.fori_loop(0, tm, drain, 0, unroll=8)
    o_ref[...] = x1_ref[...] + mod_ref[0, 5:6, :] * ybuf[slot]


def _combine_call(pos, x1, mod_l, ys, dims):
    NC, NL, S, DS = dims
    M = NC + NL
    tm = TM_COMB
    nct = NC // tm
    tpl = DS // tm

    def modrow(i, pos):
        return (jnp.where(i < nct, 0, 1 + (i - nct) // tpl), 0, 0)

    return pl.pallas_call(
        _combine_kernel,
        grid_spec=pltpu.PrefetchScalarGridSpec(
            num_scalar_prefetch=1,
            grid=(M // tm,),
            in_specs=[
                pl.BlockSpec((tm, D), lambda i, pos: (i, 0)),
                pl.BlockSpec((1, N_MOD, D), modrow),
                pl.BlockSpec(memory_space=pl.ANY),
            ],
            out_specs=pl.BlockSpec((tm, D), lambda i, pos: (i, 0)),
            scratch_shapes=[pltpu.VMEM((2, tm, D), F32), pltpu.SemaphoreType.DMA((2,))],
        ),
        out_shape=jax.ShapeDtypeStruct((M, D), F32),
        compiler_params=_cparams(("arbitrary",), 32),
        name="combine",
    )(pos, x1, mod_l, ys)


def _rope_tables(DS, tm):
    rows = DS // GRID_W
    pos_r = jnp.repeat(jnp.arange(rows, dtype=F32), GRID_W)
    pos_c = jnp.tile(jnp.arange(GRID_W, dtype=F32), rows)
    inv_freq = jnp.power(ROPE_THETA, -jnp.arange(ROPE_FREQ, dtype=F32) / ROPE_FREQ)
    ang = jnp.stack([pos_r[:, None] * inv_freq, pos_c[:, None] * inv_freq], axis=1)
    cos, sin = jnp.cos(ang), jnp.sin(ang)
    zeros = jnp.zeros_like(sin)
    c = jnp.tile(jnp.stack([cos, cos], axis=2).reshape(DS, A_DH), (1, 2))
    sm = jnp.tile(jnp.stack([-sin, zeros], axis=2).reshape(DS, A_DH), (1, 2))
    sp = jnp.tile(jnp.stack([zeros, sin], axis=2).reshape(DS, A_DH), (1, 2))
    ident = jnp.ones((tm, LANES), F32)
    zpad = jnp.zeros((tm, LANES), F32)
    return (jnp.concatenate([ident, c], 0), jnp.concatenate([zpad, sm], 0), jnp.concatenate([zpad, sp], 0))


def kernel(x_prompt, x_sample, cache_k, cache_v, c, c_ctx, w_ada, b_ada, norm1_g, norm2_g, w_in, q_norm_g,
           k_norm_g, lam_q1, lam_k1, lam_q2, lam_k2, subln_g, w_pool, pool_scale, conv_w, conv_b, w_br_a,
           w_br_p, w_br_c, w_out, w_route_group, b_route_group, w_route_expert, b_route_expert, w_exp_gate,
           w_exp_up, w_exp_down):
    B, S, _ = x_prompt.shape
    DB, DS, _ = x_sample.shape
    L = w_in.shape[0]
    P = cache_k.shape[2]
    NC, NL = B * S, DB * DS
    M = NC + NL
    dims = (NC, NL, S, DS)
    assert DB + 1 <= 8 and NC % DS == 0 and NC % TM_IN == 0 and DS % TM_IN == 0 and TM_IN % S == 0
    assert S & (S - 1) == 0 and DS & (DS - 1) == 0

    cond8 = jnp.concatenate([c_ctx[None], c, jnp.zeros((8 - 1 - DB, D), F32)], axis=0)
    mod = _ada_call(cond8, w_ada, b_ada[:, None, :]).reshape(L, 8, N_MOD, D)

    w_in_r = jnp.concatenate([w_in[:, :, 5 * COL_BLK:], w_in[:, :, :5 * COL_BLK]], axis=2).astype(BF16)
    wa, wp, wc, wo = (w.astype(BF16) for w in (w_br_a, w_br_p, w_br_c, w_out))
    wg, wu, wd = (w.astype(BF16) for w in (w_exp_gate, w_exp_up, w_exp_down))
    wpool = w_pool.astype(BF16)
    wr = jnp.concatenate([w_route_group, w_route_expert,
                          jnp.zeros((L, D, LANES - N_GROUPS - N_EXPERTS), F32)], axis=2).astype(BF16)
    br = jnp.concatenate([b_route_group, b_route_expert,
                          jnp.zeros((L, LANES - N_GROUPS - N_EXPERTS), F32)], axis=1)[:, None, :]

    rope_c, rope_sm, rope_sp = _rope_tables(DS, TM_IN)
    cache_k4 = cache_k.reshape(DB, L, P, A_WIDTH)
    cache_v4 = cache_v.reshape(DB, L, P, A_WIDTH)

    x_all = jnp.concatenate([x_prompt.reshape(NC, D), x_sample.reshape(NL, D)], axis=0)
    kc = jnp.zeros((B, L, S, A_WIDTH), F32)
    vc = jnp.zeros((B, L, S, A_WIDTH), F32)
    n_chunks = (M + N_CLASSES * (TM_MOE - 1)) // TM_MOE + 1
    xs = jnp.zeros((n_chunks * TM_MOE, XS_COLS), F32)

    for l in range(L):
        lam_init = 0.8 - 0.6 * float(np.exp(-0.3 * l))
        mod_l = mod[l]
        qg = jnp.tile(q_norm_g[l], 2)[None]
        kg = jnp.tile(k_norm_g[l], 2)[None]
        proj, kc, vc = _inproj_call(l, x_all, mod_l, norm1_g[l][None], w_in_r, qg, kg,
                                    rope_c, rope_sm, rope_sp, kc, vc, dims)
        lams = [a[l][None] for a in (lam_q1, lam_k1, lam_q2, lam_k2)]
        oa = _attention_call(l, proj, cache_k4, cache_v4, lams, subln_g[l][None], lam_init, dims)
        opc = _mix_call(proj, wpool[l], pool_scale[l][None], conv_w[l], conv_b[l][None], dims)
        x1, h2w, cls = _post_call(l, x_all, oa, opc, proj, mod_l, norm2_g[l][None],
                                  wa, wp, wc, wo, wr, br, dims)
        pos, cidx, ea, eb, valid = _sort_plan(cls[:, 0], n_chunks)
        xs = _dispatch_call(pos, h2w, xs)
        ys = _moe_call(l, cidx, ea, eb, valid, xs, wg, wu, wd)
        x_all = _combine_call(pos, x1, mod_l, ys, dims)

    y_p = x_all[:NC].reshape(B, S, D)
    y_s = x_all[NC:].reshape(DB, DS, D)
    return (y_p, y_s, kc.reshape(B, L, S, A_HEADS, 2, A_DH), vc.reshape(B, L, S, A_HEADS, A_DV))
```

```python
import functools
import math

import numpy as np
import jax
import jax.numpy as jnp
from jax import lax
from jax.experimental import pallas as pl
from jax.experimental.pallas import tpu as pltpu

F32 = jnp.float32
BF16 = jnp.bfloat16
I32 = jnp.int32

D = 2048
N_MOD = 6
EPS = 1e-6
GRID_W = 64
A_WIDTH = 1024
A_HEADS = 8
A_DH = 64
A_DV = 128
ROPE_FREQ = 16
ROPE_THETA = 10000.0
P_WIDTH = 512
P_GROUPS = 4
P_GC = 128
P_WINDOWS = (2, 4, 8, 16)
C_WIDTH = 512
N_GROUPS = 4
E_PER_GROUP = 4
N_EXPERTS = 16
D_EXPERT = 512

LANES = 128
PROJ_COLS = 11264
COL_BLK = 1024
N_GATE_BLKS = 6
J_Q, J_K, J_V, J_PC, J_BC = 6, 7, 8, 9, 10
W_IN_BLKS = PROJ_COLS // COL_BLK
N_PAIRS = 6
N_CLASSES = N_GROUPS * N_PAIRS
XS_COLS = D + LANES

TM_IN = 512
TQ = 256
TM_MIX = 2048
MIX_BLK = 256
TM_POST = 256
TM_MOE = 256
TM_COMB = 256

_PAIRS = [(a, b) for a in range(E_PER_GROUP) for b in range(a + 1, E_PER_GROUP)]
_CLASS_EA = np.array([g * E_PER_GROUP + _PAIRS[p][0] for g in range(N_GROUPS) for p in range(N_PAIRS)], np.int32)
_CLASS_EB = np.array([g * E_PER_GROUP + _PAIRS[p][1] for g in range(N_GROUPS) for p in range(N_PAIRS)], np.int32)


def _cparams(sem, vmem_mb):
    return pltpu.CompilerParams(dimension_semantics=sem, vmem_limit_bytes=vmem_mb * 1024 * 1024)


def _dot(a, b):
    return jnp.dot(a, b, preferred_element_type=F32)


def _dot_nt(a, b):
    return lax.dot_general(a, b, (((1,), (1,)), ((), ())), preferred_element_type=F32)


def _ada_kernel(c_ref, w_ref, b_ref, o_ref):
    c = c_ref[...]
    a = (c * jax.nn.sigmoid(c)).astype(BF16)
    o_ref[0] = _dot(a, w_ref[0].astype(BF16)) + b_ref[0]


def _ada_call(cond8, w_ada, b_ada3):
    L = w_ada.shape[0]
    tn = 1024
    return pl.pallas_call(
        _ada_kernel,
        grid=(L, N_MOD * D // tn),
        in_specs=[
            pl.BlockSpec((8, D), lambda l, j: (0, 0)),
            pl.BlockSpec((1, D, tn), lambda l, j: (l, 0, j)),
            pl.BlockSpec((1, 1, tn), lambda l, j: (l, 0, j)),
        ],
        out_specs=pl.BlockSpec((1, 8, tn), lambda l, j: (l, 0, j)),
        out_shape=jax.ShapeDtypeStruct((L, 8, N_MOD * D), F32),
        compiler_params=_cparams(("arbitrary", "arbitrary"), 40),
        name="ada_mod",
    )(cond8, w_ada, b_ada3)


def _norm_rope(x, tab_ref, t0):
    lane = lax.broadcasted_iota(I32, x.shape, 1)
    lo = lane < A_DH
    sq = x * x
    s_lo = jnp.sum(jnp.where(lo, sq, 0.0), axis=-1, keepdims=True)
    s_hi = jnp.sum(jnp.where(lo, 0.0, sq), axis=-1, keepdims=True)
    rs = lax.rsqrt(jnp.where(lo, s_lo, s_hi) * (1.0 / A_DH) + EPS)
    up = pltpu.roll(x, LANES - ROPE_FREQ, 1)
    dn = pltpu.roll(x, ROPE_FREQ, 1)
    return (x * tab_ref[t0] + up * tab_ref[t0 + 1] + dn * tab_ref[t0 + 2]) * rs


def _inproj_kernel(*refs, n_ctx_tiles, aliased):
    if aliased:
        x_ref, mod_ref, g1_ref, w_ref, tab_ref, _, _, proj_ref, kc_ref, vc_ref, h_ref = refs
    else:
        x_ref, mod_ref, g1_ref, w_ref, tab_ref, proj_ref, kc_ref, vc_ref, h_ref = refs
    i = pl.program_id(0)
    j = pl.program_id(1)

    def matmul():
        return _dot(h_ref[...], w_ref[0])

    @pl.when(j == 0)
    def _():
        x = x_ref[...]
        ms = jnp.mean(x * x, axis=-1, keepdims=True)
        y = x * lax.rsqrt(ms + EPS) * g1_ref[...]
        h = y * (1.0 + mod_ref[0, 1:2, :]) + mod_ref[0, 0:1, :]
        h_ref[...] = h.astype(BF16)
        proj_ref[...] = jax.nn.sigmoid(matmul()).astype(BF16)

    @pl.when(jnp.logical_and(j > 0, j < N_GATE_BLKS))
    def _():
        proj_ref[...] = jax.nn.sigmoid(matmul()).astype(BF16)

    @pl.when(j == J_Q)
    def _():
        acc = matmul()
        for h in range(A_HEADS):
            sl = slice(h * LANES, (h + 1) * LANES)
            proj_ref[:, sl] = _norm_rope(acc[:, sl], tab_ref, 0).astype(BF16)

    @pl.when(j == J_K)
    def _():
        acc = matmul()
        for h in range(A_HEADS):
            sl = slice(h * LANES, (h + 1) * LANES)
            y = _norm_rope(acc[:, sl], tab_ref, 3)
            proj_ref[:, sl] = y.astype(BF16)

            @pl.when(i < n_ctx_tiles)
            def _():
                kc_ref[:, 0, :, sl] = y.reshape(kc_ref.shape[0], kc_ref.shape[2], LANES)

    @pl.when(j == J_V)
    def _():
        acc = matmul()
        proj_ref[...] = acc.astype(BF16)

        @pl.when(i < n_ctx_tiles)
        def _():
            vc_ref[:, 0, :, :] = acc.reshape(vc_ref.shape[0], vc_ref.shape[2], A_WIDTH)

    @pl.when(j > J_V)
    def _():
        proj_ref[...] = matmul().astype(BF16)


def _inproj_call(l, x_all, mod_l, g1, w_in_b, tabs, kc, vc, cache_shape, dims):
    NC, NL, S, DS = dims
    M = NC + NL
    tm = TM_IN
    nct = NC // tm
    tpl = DS // tm
    seq_per_tile = tm // S
    aliased = kc is not None

    def modrow(i):
        return jnp.where(i < nct, 0, 1 + (i - nct) // tpl)

    def roperow(i):
        return jnp.where(i < nct, 0, 1 + (i - nct) % tpl)

    def wcol(j):
        return jnp.where(j < N_GATE_BLKS, j + (W_IN_BLKS - N_GATE_BLKS), j - N_GATE_BLKS)

    def cache_idx(i, j):
        return (jnp.minimum(i, nct - 1), l, 0, 0)

    in_specs = [
        pl.BlockSpec((tm, D), lambda i, j: (i, 0)),
        pl.BlockSpec((1, N_MOD, D), lambda i, j: (modrow(i), 0, 0)),
        pl.BlockSpec((1, D), lambda i, j: (0, 0)),
        pl.BlockSpec((1, D, COL_BLK), lambda i, j: (l, 0, wcol(j))),
        pl.BlockSpec((6, tm, LANES), lambda i, j: (0, roperow(i), 0)),
    ]
    args = [x_all, mod_l, g1, w_in_b, tabs]
    aliases = {}
    if aliased:
        in_specs += [pl.BlockSpec(memory_space=pl.ANY), pl.BlockSpec(memory_space=pl.ANY)]
        args += [kc, vc]
        aliases = {5: 1, 6: 2}
    kern = functools.partial(_inproj_kernel, n_ctx_tiles=nct, aliased=aliased)
    return pl.pallas_call(
        kern,
        grid=(M // tm, PROJ_COLS // COL_BLK),
        in_specs=in_specs,
        out_specs=[
            pl.BlockSpec((tm, COL_BLK), lambda i, j: (i, j)),
            pl.BlockSpec((seq_per_tile, 1, S, A_WIDTH), cache_idx),
            pl.BlockSpec((seq_per_tile, 1, S, A_WIDTH), cache_idx),
        ],
        out_shape=[
            jax.ShapeDtypeStruct((M, PROJ_COLS), BF16),
            jax.ShapeDtypeStruct(cache_shape, F32),
            jax.ShapeDtypeStruct(cache_shape, F32),
        ],
        scratch_shapes=[pltpu.VMEM((tm, D), BF16)],
        input_output_aliases=aliases,
        compiler_params=_cparams(("arbitrary", "arbitrary"), 48),
        name="in_proj",
    )(*args)


def _lambda(lq1, lk1, lq2, lk2, lam_init):
    a = jnp.sum(lq1[...] * lk1[...], axis=-1, keepdims=True)
    b = jnp.sum(lq2[...] * lk2[...], axis=-1, keepdims=True)
    return jnp.exp(a) - jnp.exp(b) + lam_init


def _attn_head(q_h, ks, vs, lam, sg, lam_init):
    lane = lax.broadcasted_iota(I32, q_h.shape, 1)
    lo = lane < A_DH
    zero = jnp.zeros_like(q_h)
    q0 = jnp.where(lo, q_h, zero)
    q1 = jnp.where(lo, zero, q_h)

    def soft(qm):
        ss = [_dot_nt(qm, k) for k in ks]
        m = functools.reduce(jnp.maximum, [jnp.max(s, axis=-1, keepdims=True) for s in ss])
        es = [jnp.exp(s - m) for s in ss]
        den = functools.reduce(lambda a, b: a + b, [jnp.sum(e, axis=-1, keepdims=True) for e in es])
        return es, den

    e0, l0 = soft(q0)
    e1, l1 = soft(q1)
    r0 = 1.0 / l0
    r1 = lam / l1
    o = None
    for a, b, v in zip(e0, e1, vs):
        part = _dot((a * r0 - b * r1).astype(BF16), v)
        o = part if o is None else o + part
    ms = jnp.mean(o * o, axis=-1, keepdims=True)
    return (o * lax.rsqrt(ms + EPS) * sg) * (1.0 - lam_init)


def _attn_ctx_kernel(q_ref, k_ref, v_ref, lq1, lk1, lq2, lk2, sg_ref, o_ref, *, lam_init):
    lam = _lambda(lq1, lk1, lq2, lk2, lam_init)
    for h in range(A_HEADS):
        sl = slice(h * LANES, (h + 1) * LANES)
        o = _attn_head(q_ref[:, sl], [k_ref[:, sl]], [v_ref[:, sl]], lam, sg_ref[...], lam_init)
        o_ref[:, sl] = o.astype(BF16)


def _attn_lat_kernel(q_ref, k_ref, v_ref, ck_ref, cv_ref, lq1, lk1, lq2, lk2, sg_ref, oin_ref, o_ref,
                     *, lam_init):
    del oin_ref
    lam = _lambda(lq1, lk1, lq2, lk2, lam_init)
    for h in range(A_HEADS):
        sl = slice(h * LANES, (h + 1) * LANES)
        ks = [k_ref[:, sl], ck_ref[0, 0, :, sl].astype(BF16)]
        vs = [v_ref[:, sl], cv_ref[0, 0, :, sl].astype(BF16)]
        o = _attn_head(q_ref[:, sl], ks, vs, lam, sg_ref[...], lam_init)
        o_ref[:, sl] = o.astype(BF16)


def _attention_call(l, proj, cache_k4, cache_v4, lams, sg, lam_init, dims):
    NC, NL, S, DS = dims
    M = NC + NL
    B = NC // S
    DB = NL // DS
    lam_specs1 = [pl.BlockSpec((1, A_DH), lambda b: (0, 0))] * 4
    oa = pl.pallas_call(
        functools.partial(_attn_ctx_kernel, lam_init=lam_init),
        grid=(B,),
        in_specs=[
            pl.BlockSpec((S, A_WIDTH), lambda b: (b, J_Q)),
            pl.BlockSpec((S, A_WIDTH), lambda b: (b, J_K)),
            pl.BlockSpec((S, A_WIDTH), lambda b: (b, J_V)),
            *lam_specs1,
            pl.BlockSpec((1, LANES), lambda b: (0, 0)),
        ],
        out_specs=pl.BlockSpec((S, A_WIDTH), lambda b: (b, 0)),
        out_shape=jax.ShapeDtypeStruct((M, A_WIDTH), BF16),
        compiler_params=_cparams(("arbitrary",), 32),
        name="attn_ctx",
    )(proj, proj, proj, *lams, sg)

    nq = DS // TQ
    q0 = NC // TQ
    k0 = NC // DS
    P = cache_k4.shape[2]
    lam_specs2 = [pl.BlockSpec((1, A_DH), lambda b, qi: (0, 0))] * 4
    oa = pl.pallas_call(
        functools.partial(_attn_lat_kernel, lam_init=lam_init),
        grid=(DB, nq),
        in_specs=[
            pl.BlockSpec((TQ, A_WIDTH), lambda b, qi: (q0 + b * nq + qi, J_Q)),
            pl.BlockSpec((DS, A_WIDTH), lambda b, qi: (k0 + b, J_K)),
            pl.BlockSpec((DS, A_WIDTH), lambda b, qi: (k0 + b, J_V)),
            pl.BlockSpec((1, 1, P, A_WIDTH), lambda b, qi: (b, l, 0, 0)),
            pl.BlockSpec((1, 1, P, A_WIDTH), lambda b, qi: (b, l, 0, 0)),
            *lam_specs2,
            pl.BlockSpec((1, LANES), lambda b, qi: (0, 0)),
            pl.BlockSpec(memory_space=pl.ANY),
        ],
        out_specs=pl.BlockSpec((TQ, A_WIDTH), lambda b, qi: (q0 + b * nq + qi, 0)),
        out_shape=jax.ShapeDtypeStruct((M, A_WIDTH), BF16),
        input_output_aliases={10: 0},
        compiler_params=_cparams(("arbitrary", "arbitrary"), 48),
        name="attn_lat",
    )(proj, proj, proj, cache_k4, cache_v4, *lams, sg, oa)
    return oa


def _mix_kernel(pc_ref, bc_ref, wp_ref, ps_ref, cw_ref, cb_ref, o_ref, *, n_ctx_tiles, S, DS):
    i = pl.program_id(0)
    tm = pc_ref.shape[0]
    nb = tm // MIX_BLK
    is_lat = i >= n_ctx_tiles
    seqlen = jnp.where(is_lat, DS, S)
    rows = lax.broadcasted_iota(I32, (tm, 1), 0)
    pos = rows & (seqlen - 1)

    t = lax.broadcasted_iota(I32, (MIX_BLK, MIX_BLK), 0)
    s = lax.broadcasted_iota(I32, (MIX_BLK, MIX_BLK), 1)
    for g, win in enumerate(P_WINDOWS):
        half = win // 2
        sl = slice(g * P_GC, (g + 1) * P_GC)
        d_cur = s - t
        band_cur = jnp.logical_and(d_cur >= -half, d_cur < half).astype(BF16)
        d_prev = d_cur - MIX_BLK
        band_prev = jnp.logical_and(is_lat, jnp.logical_and(d_prev >= -half, d_prev < half)).astype(BF16)
        d_next = d_cur + MIX_BLK
        band_next = jnp.logical_and(is_lat, jnp.logical_and(d_next >= -half, d_next < half)).astype(BF16)
        for b in range(nb):
            r0 = b * MIX_BLK
            u_b = pc_ref[r0:r0 + MIX_BLK, sl]
            acc = _dot(band_cur, u_b)
            if b > 0:
                acc = acc + _dot(band_prev, pc_ref[r0 - MIX_BLK:r0, sl])
            if b + 1 < nb:
                acc = acc + _dot(band_next, pc_ref[r0 + MIX_BLK:r0 + 2 * MIX_BLK, sl])
            p = pos[r0:r0 + MIX_BLK]
            inv_cnt = 1.0 / (jnp.minimum(p + half, seqlen) - jnp.maximum(p - half, 0)).astype(F32)
            y = (acc * inv_cnt - u_b.astype(F32)).astype(BF16)
            o_ref[r0:r0 + MIX_BLK, sl] = (_dot(y, wp_ref[g]) * ps_ref[:, sl]).astype(BF16)

    u = pc_ref[:, P_WIDTH:].astype(F32)
    gate_b = bc_ref[:, :C_WIDTH].astype(F32)
    gate_c = bc_ref[:, C_WIDTH:].astype(F32)
    z = gate_c * u
    zm = jnp.where(pos >= 1, pltpu.roll(z, 1, 0), 0.0)
    zp = jnp.where(pos + 1 < seqlen, pltpu.roll(z, tm - 1, 0), 0.0)
    conv = zm * cw_ref[0:1, :] + z * cw_ref[1:2, :] + zp * cw_ref[2:3, :] + cb_ref[...]
    o_ref[:, P_WIDTH:] = (gate_b * conv).astype(BF16)


def _mix_call(proj, w_pool_l, pool_scale_l, conv_w_l, conv_b_l, dims):
    NC, NL, S, DS = dims
    M = NC + NL
    tm = TM_MIX
    assert S == MIX_BLK and tm == DS and NC % tm == 0 and max(P_WINDOWS) // 2 <= MIX_BLK
    kern = functools.partial(_mix_kernel, n_ctx_tiles=NC // tm, S=S, DS=DS)
    return pl.pallas_call(
        kern,
        grid=(M // tm,),
        in_specs=[
            pl.BlockSpec((tm, COL_BLK), lambda i: (i, J_PC)),
            pl.BlockSpec((tm, COL_BLK), lambda i: (i, J_BC)),
            pl.BlockSpec((P_GROUPS, P_GC, P_GC), lambda i: (0, 0, 0)),
            pl.BlockSpec((1, P_WIDTH), lambda i: (0, 0)),
            pl.BlockSpec((3, C_WIDTH), lambda i: (0, 0)),
            pl.BlockSpec((1, C_WIDTH), lambda i: (0, 0)),
        ],
        out_specs=pl.BlockSpec((tm, P_WIDTH + C_WIDTH), lambda i: (i, 0)),
        out_shape=jax.ShapeDtypeStruct((M, P_WIDTH + C_WIDTH), BF16),
        compiler_params=_cparams(("arbitrary",), 48),
        name="mixers",
    )(proj, proj, w_pool_l, pool_scale_l, conv_w_l, conv_b_l)


def _route(logits):
    lane = lax.broadcasted_iota(I32, logits.shape, 1)
    lane_f = lane.astype(F32)
    neg = -jnp.inf

    def first_argmax(v, vmax):
        return jnp.min(jnp.where(v == vmax, lane_f, float(LANES)), axis=-1, keepdims=True).astype(I32)

    gl = jnp.where(lane < N_GROUPS, logits, neg)
    gmax = jnp.max(gl, axis=-1, keepdims=True)
    gsel = first_argmax(gl, gmax)
    g_w = 1.0 / jnp.sum(jnp.exp(gl - gmax), axis=-1, keepdims=True)

    first = N_GROUPS + gsel * E_PER_GROUP
    in_grp = jnp.logical_and(lane >= first, lane < first + E_PER_GROUP)
    el = jnp.where(in_grp, logits, neg)
    t1 = jnp.max(el, axis=-1, keepdims=True)
    i1 = first_argmax(el, t1)
    el2 = jnp.where(lane == i1, neg, el)
    t2 = jnp.max(el2, axis=-1, keepdims=True)
    i2 = first_argmax(el2, t2)
    a = jnp.exp(t2 - t1)
    w1 = g_w / (1.0 + a)
    w2 = g_w * a / (1.0 + a)

    swap = i2 < i1
    llo = jnp.where(swap, i2, i1) - first
    lhi = jnp.where(swap, i1, i2) - first
    w_lo = jnp.where(swap, w2, w1)
    w_hi = jnp.where(swap, w1, w2)
    pair = ((llo * (7 - llo)) >> 1) + (lhi - llo - 1)
    cls = gsel * N_PAIRS + pair
    wvec = jnp.where(lane == 0, w_lo, jnp.where(lane == 1, w_hi, 0.0))
    return cls, wvec


def _post_kernel(x_ref, oa_ref, opc_ref, g0_ref, g1_ref, g2_ref, mod_ref, n2_ref, wa_ref, wp_ref, wc_ref,
                 wo_ref, wr_ref, br_ref, x1_ref, h2w_ref, cls_ref):
    a = _dot(oa_ref[...], wa_ref[0])
    merged = g0_ref[...].astype(F32) * a
    p = _dot(opc_ref[:, :P_WIDTH], wp_ref[0])
    merged = merged + g1_ref[...].astype(F32) * p
    c = _dot(opc_ref[:, P_WIDTH:], wc_ref[0])
    merged = merged + g2_ref[...].astype(F32) * c
    y = _dot(merged.astype(BF16), wo_ref[0])
    x1 = x_ref[...] + mod_ref[0, 2:3, :] * y
    x1_ref[...] = x1
    ms = jnp.mean(x1 * x1, axis=-1, keepdims=True)
    h2 = x1 * lax.rsqrt(ms + EPS) * n2_ref[...]
    h2 = h2 * (1.0 + mod_ref[0, 4:5, :]) + mod_ref[0, 3:4, :]
    h2w_ref[:, :D] = h2
    logits = _dot(h2.astype(BF16), wr_ref[0]) + br_ref[0]
    cls, wvec = _route(logits)
    h2w_ref[:, D:] = wvec
    cls_ref[...] = jnp.broadcast_to(cls, cls_ref.shape)


def _post_call(l, x_all, oa, opc, proj, mod_l, g2, wa, wp, wc, wo, wr, br, dims):
    NC, NL, S, DS = dims
    M = NC + NL
    tm = TM_POST
    nct = NC // tm
    tpl = DS // tm

    def modrow(i):
        return jnp.where(i < nct, 0, 1 + (i - nct) // tpl)

    const3 = lambda i: (l, 0, 0)
    single = pl.Buffered(1)
    return pl.pallas_call(
        _post_kernel,
        grid=(M // tm,),
        in_specs=[
            pl.BlockSpec((tm, D), lambda i: (i, 0)),
            pl.BlockSpec((tm, A_WIDTH), lambda i: (i, 0)),
            pl.BlockSpec((tm, P_WIDTH + C_WIDTH), lambda i: (i, 0)),
            pl.BlockSpec((tm, D), lambda i: (i, 0)),
            pl.BlockSpec((tm, D), lambda i: (i, 1)),
            pl.BlockSpec((tm, D), lambda i: (i, 2)),
            pl.BlockSpec((1, N_MOD, D), lambda i: (modrow(i), 0, 0)),
            pl.BlockSpec((1, D), lambda i: (0, 0)),
            pl.BlockSpec((1, A_WIDTH, D), const3, pipeline_mode=single),
            pl.BlockSpec((1, P_WIDTH, D), const3, pipeline_mode=single),
            pl.BlockSpec((1, C_WIDTH, D), const3, pipeline_mode=single),
            pl.BlockSpec((1, D, D), const3, pipeline_mode=single),
            pl.BlockSpec((1, D, LANES), const3, pipeline_mode=single),
            pl.BlockSpec((1, 1, LANES), const3),
        ],
        out_specs=[
            pl.BlockSpec((tm, D), lambda i: (i, 0)),
            pl.BlockSpec((tm, XS_COLS), lambda i: (i, 0)),
            pl.BlockSpec((tm, LANES), lambda i: (i, 0)),
        ],
        out_shape=[
            jax.ShapeDtypeStruct((M, D), F32),
            jax.ShapeDtypeStruct((M, XS_COLS), F32),
            jax.ShapeDtypeStruct((M, LANES), I32),
        ],
        compiler_params=_cparams(("arbitrary",), 52),
        name="post_attn",
    )(x_all, oa, opc, proj, proj, proj, mod_l, g2, wa, wp, wc, wo, wr, br)


def _sort_plan(cls, n_chunks):
    tm = TM_MOE
    M = cls.shape[0]
    onehot = (cls[:, None] == jnp.arange(N_CLASSES, dtype=I32)[None, :]).astype(I32)
    csum = jnp.cumsum(onehot, axis=0)
    rank = jnp.sum(onehot * csum, axis=1) - 1
    counts = csum[-1]
    padded = ((counts + tm - 1) // tm) * tm
    ends = jnp.cumsum(padded)
    starts = ends - padded
    pos = (jnp.sum(onehot * starts[None, :], axis=1) + rank).astype(I32)
    used = ends[-1] // tm
    chunk = jnp.arange(n_chunks, dtype=I32)
    cidx = jnp.minimum(chunk, used - 1)
    ccls = jnp.sum((ends[None, :] <= (cidx * tm)[:, None]).astype(I32), axis=1)
    ea = jnp.asarray(_CLASS_EA)[ccls]
    eb = jnp.asarray(_CLASS_EB)[ccls]
    src = jnp.zeros((n_chunks * tm,), I32).at[pos].set(jnp.arange(M, dtype=I32), unique_indices=True)
    return pos, src, cidx.astype(I32), ea, eb, used.astype(I32)[None]


def _moe_kernel(src_ref, cidx_ref, ea_ref, eb_ref, used_ref, h2w_ref, wga, wua, wda, wgb, wub, wdb, ys_ref,
                xbuf, sem):
    del cidx_ref, ea_ref, eb_ref
    tm = ys_ref.shape[0]
    c = pl.program_id(0)
    used = used_ref[0]

    def row_copy(chunk, slot, r):
        src = src_ref[chunk * tm + r]
        return pltpu.make_async_copy(h2w_ref.at[pl.ds(src, 1)], xbuf.at[slot, pl.ds(r, 1)], sem.at[slot])

    def wait_rows(slot):
        def body(r, carry):
            pltpu.make_async_copy(h2w_ref.at[pl.ds(0, 1)], xbuf.at[slot, pl.ds(0, 1)], sem.at[slot]).wait()
            return carry
        lax.fori_loop(0, tm, body, 0, unroll=8)

    @pl.when(c == 0)
    def _():
        def body(r, carry):
            row_copy(0, 0, r).start()
            return carry
        lax.fori_loop(0, tm, body, 0, unroll=8)

    @pl.when(c < used)
    def _():
        slot = c % 2
        wait_rows(slot)
        nxt = jnp.minimum(c + 1, used - 1)
        for r in range(tm):
            row_copy(nxt, 1 - slot, r).start()

        x = xbuf[slot, :, :D].astype(BF16)
        wts = xbuf[slot, :, D:]

        def expert(wg, wu, wd, w):
            g = _dot(x, wg[0, 0])
            u = _dot(x, wu[0, 0])
            act = (g * jax.nn.sigmoid(g)) * u
            return _dot((act * w).astype(BF16), wd[0, 0])

        ys_ref[...] = expert(wga, wua, wda, wts[:, 0:1]) + expert(wgb, wub, wdb, wts[:, 1:2])

        @pl.when(c + 1 >= used)
        def _():
            wait_rows(1 - slot)


def _moe_call(l, src, cidx, ea, eb, used, h2w, wg, wu, wd):
    n_chunks = cidx.shape[0]
    tm = TM_MOE
    up_a = pl.BlockSpec((1, 1, D, D_EXPERT), lambda c, s, ci, a, b, u: (l, a[c], 0, 0))
    up_b = pl.BlockSpec((1, 1, D, D_EXPERT), lambda c, s, ci, a, b, u: (l, b[c], 0, 0))
    dn_a = pl.BlockSpec((1, 1, D_EXPERT, D), lambda c, s, ci, a, b, u: (l, a[c], 0, 0))
    dn_b = pl.BlockSpec((1, 1, D_EXPERT, D), lambda c, s, ci, a, b, u: (l, b[c], 0, 0))
    return pl.pallas_call(
        _moe_kernel,
        grid_spec=pltpu.PrefetchScalarGridSpec(
            num_scalar_prefetch=5,
            grid=(n_chunks,),
            in_specs=[
                pl.BlockSpec(memory_space=pl.ANY),
                up_a, up_a, dn_a, up_b, up_b, dn_b,
            ],
            out_specs=pl.BlockSpec((tm, D), lambda c, s, ci, a, b, u: (ci[c], 0)),
            scratch_shapes=[pltpu.VMEM((2, tm, XS_COLS), F32), pltpu.SemaphoreType.DMA((2,))],
        ),
        out_shape=jax.ShapeDtypeStruct((n_chunks * tm, D), F32),
        compiler_params=_cparams(("arbitrary",), 48),
        name="moe_experts",
    )(src, cidx, ea, eb, used, h2w, wg, wu, wd, wg, wu, wd)


def _combine_kernel(pos_ref, x1_ref, mod_ref, ys_ref, *rest, n_ctx_tiles, split):
    if split:
        op_ref, os_ref, ybuf, sem = rest
    else:
        o_ref, ybuf, sem = rest
    tm = x1_ref.shape[0]
    i = pl.program_id(0)
    n = pl.num_programs(0)

    def issue(step, slot):
        base = step * tm

        def body(r, carry):
            src = pos_ref[base + r]
            pltpu.make_async_copy(ys_ref.at[pl.ds(src, 1)], ybuf.at[slot, pl.ds(r, 1)], sem.at[slot]).start()
            return carry

        lax.fori_loop(0, tm, body, 0, unroll=8)

    @pl.when(i == 0)
    def _():
        issue(0, 0)

    @pl.when(i + 1 < n)
    def _():
        issue(i + 1, (i + 1) % 2)

    slot = i % 2

    def drain(r, carry):
        pltpu.make_async_copy(ys_ref.at[pl.ds(0, 1)], ybuf.at[slot, pl.ds(0, 1)], sem.at[slot]).wait()
        return carry

    lax.fori_loop(0, tm, drain, 0, unroll=8)
    out = x1_ref[...] + mod_ref[0, 5:6, :] * ybuf[slot]
    if split:
        @pl.when(i < n_ctx_tiles)
        def _():
            op_ref[...] = out

        @pl.when(i >= n_ctx_tiles)
        def _():
            os_ref[...] = out
    else:
        o_ref[...] = out


def _combine_call(pos, x1, mod_l, ys, dims, split):
    NC, NL, S, DS = dims
    M = NC + NL
    tm = TM_COMB
    nct = NC // tm
    tpl = DS // tm

    def modrow(i, pos):
        return (jnp.where(i < nct, 0, 1 + (i - nct) // tpl), 0, 0)

    if split:
        out_specs = [pl.BlockSpec((tm, D), lambda i, pos: (jnp.minimum(i, nct - 1), 0)),
                     pl.BlockSpec((tm, D), lambda i, pos: (jnp.maximum(i - nct, 0), 0))]
        out_shape = [jax.ShapeDtypeStruct((NC, D), F32), jax.ShapeDtypeStruct((NL, D), F32)]
    else:
        out_specs = pl.BlockSpec((tm, D), lambda i, pos: (i, 0))
        out_shape = jax.ShapeDtypeStruct((M, D), F32)
    return pl.pallas_call(
        functools.partial(_combine_kernel, n_ctx_tiles=nct, split=split),
        grid_spec=pltpu.PrefetchScalarGridSpec(
            num_scalar_prefetch=1,
            grid=(M // tm,),
            in_specs=[
                pl.BlockSpec((tm, D), lambda i, pos: (i, 0)),
                pl.BlockSpec((1, N_MOD, D), modrow),
                pl.BlockSpec(memory_space=pl.ANY),
            ],
            out_specs=out_specs,
            scratch_shapes=[pltpu.VMEM((2, tm, D), F32), pltpu.SemaphoreType.DMA((2,))],
        ),
        out_shape=out_shape,
        compiler_params=_cparams(("arbitrary",), 32),
        name="combine",
    )(pos, x1, mod_l, ys)


def _rope_tables(DS, tm):
    rows = DS // GRID_W
    pos_r = jnp.repeat(jnp.arange(rows, dtype=F32), GRID_W)
    pos_c = jnp.tile(jnp.arange(GRID_W, dtype=F32), rows)
    inv_freq = jnp.power(ROPE_THETA, -jnp.arange(ROPE_FREQ, dtype=F32) / ROPE_FREQ)
    ang = jnp.stack([pos_r[:, None] * inv_freq, pos_c[:, None] * inv_freq], axis=1)
    cos, sin = jnp.cos(ang), jnp.sin(ang)
    zeros = jnp.zeros_like(sin)
    c = jnp.tile(jnp.stack([cos, cos], axis=2).reshape(DS, A_DH), (1, 2))
    sm = jnp.tile(jnp.stack([-sin, zeros], axis=2).reshape(DS, A_DH), (1, 2))
    sp = jnp.tile(jnp.stack([zeros, sin], axis=2).reshape(DS, A_DH), (1, 2))
    ident = jnp.ones((tm, LANES), F32)
    zpad = jnp.zeros((tm, LANES), F32)
    return (jnp.concatenate([ident, c], 0), jnp.concatenate([zpad, sm], 0), jnp.concatenate([zpad, sp], 0))


def _qk_tables(rope, q_gain, k_gain):
    c, sm, sp = rope
    out = []
    for gain, scale in ((q_gain, A_DH ** -0.5), (k_gain, 1.0)):
        g = jnp.tile(gain, 2) * scale
        out += [c * g, sm * jnp.roll(g, -ROPE_FREQ), sp * jnp.roll(g, ROPE_FREQ)]
    return jnp.stack(out, axis=0)


def kernel(x_prompt, x_sample, cache_k, cache_v, c, c_ctx, w_ada, b_ada, norm1_g, norm2_g, w_in, q_norm_g,
           k_norm_g, lam_q1, lam_k1, lam_q2, lam_k2, subln_g, w_pool, pool_scale, conv_w, conv_b, w_br_a,
           w_br_p, w_br_c, w_out, w_route_group, b_route_group, w_route_expert, b_route_expert, w_exp_gate,
           w_exp_up, w_exp_down):
    B, S, _ = x_prompt.shape
    DB, DS, _ = x_sample.shape
    L = w_in.shape[0]
    P = cache_k.shape[2]
    NC, NL = B * S, DB * DS
    M = NC + NL
    dims = (NC, NL, S, DS)
    assert DB + 1 <= 8 and NC % DS == 0 and NC % TM_IN == 0 and DS % TM_IN == 0 and TM_IN % S == 0
    assert S & (S - 1) == 0 and DS & (DS - 1) == 0

    cond8 = jnp.concatenate([c_ctx[None], c, jnp.zeros((8 - 1 - DB, D), F32)], axis=0)
    mod = _ada_call(cond8, w_ada, b_ada[:, None, :]).reshape(L, 8, N_MOD, D)

    w_in_b = w_in.astype(BF16)
    wa, wp, wc, wo = (w.astype(BF16) for w in (w_br_a, w_br_p, w_br_c, w_out))
    wg, wu, wd = (w.astype(BF16) for w in (w_exp_gate, w_exp_up, w_exp_down))
    wpool = w_pool.astype(BF16)
    wr = jnp.concatenate([w_route_group, w_route_expert,
                          jnp.zeros((L, D, LANES - N_GROUPS - N_EXPERTS), F32)], axis=2).astype(BF16)
    br = jnp.concatenate([b_route_group, b_route_expert,
                          jnp.zeros((L, LANES - N_GROUPS - N_EXPERTS), F32)], axis=1)[:, None, :]

    rope = _rope_tables(DS, TM_IN)
    cache_k4 = cache_k.reshape(DB, L, P, A_WIDTH)
    cache_v4 = cache_v.reshape(DB, L, P, A_WIDTH)

    x_all = jnp.concatenate([x_prompt.reshape(NC, D), x_sample.reshape(NL, D)], axis=0)
    kc = vc = None
    n_chunks = (M + N_CLASSES * (TM_MOE - 1)) // TM_MOE + 1

    for l in range(L):
        lam_init = 0.8 - 0.6 * math.exp(-0.3 * l)
        mod_l = mod[l]
        tabs = _qk_tables(rope, q_norm_g[l], k_norm_g[l])
        proj, kc, vc = _inproj_call(l, x_all, mod_l, norm1_g[l][None], w_in_b, tabs, kc, vc,
                                    (B, L, S, A_WIDTH), dims)
        lams = [a[l][None] for a in (lam_q1, lam_k1, lam_q2, lam_k2)]
        oa = _attention_call(l, proj, cache_k4, cache_v4, lams, subln_g[l][None], lam_init, dims)
        opc = _mix_call(proj, wpool[l], pool_scale[l][None], conv_w[l], conv_b[l][None], dims)
        x1, h2w, cls = _post_call(l, x_all, oa, opc, proj, mod_l, norm2_g[l][None],
                                  wa, wp, wc, wo, wr, br, dims)
        pos, src, cidx, ea, eb, used = _sort_plan(cls[:, 0], n_chunks)
        ys = _moe_call(l, src, cidx, ea, eb, used, h2w, wg, wu, wd)
        x_all = _combine_call(pos, x1, mod_l, ys, dims, split=(l == L - 1))

    y_p, y_s = x_all
    return (y_p.reshape(B, S, D), y_s.reshape(DB, DS, D),
            kc.reshape(B, L, S, A_HEADS, 2, A_DH), vc.reshape(B, L, S, A_HEADS, A_DV))
```

```python
import functools
import math

import numpy as np
import jax
import jax.numpy as jnp
from jax import lax
from jax.experimental import pallas as pl
from jax.experimental.pallas import tpu as pltpu

F32 = jnp.float32
BF16 = jnp.bfloat16
I32 = jnp.int32
U32 = jnp.uint32

D = 2048
N_MOD = 6
EPS = 1e-6
GRID_W = 64
A_WIDTH = 1024
A_HEADS = 8
A_DH = 64
A_DV = 128
ROPE_FREQ = 16
ROPE_THETA = 10000.0
P_WIDTH = 512
P_GROUPS = 4
P_GC = 128
P_WINDOWS = (2, 4, 8, 16)
C_WIDTH = 512
N_GROUPS = 4
E_PER_GROUP = 4
N_EXPERTS = 16
D_EXPERT = 512

LANES = 128
MXU_DIM = 256
PROJ_COLS = 11264
COL_BLK = 1024
N_GATE_BLKS = 6
J_Q, J_K, J_V, J_PC, J_BC = 6, 7, 8, 9, 10
W_IN_BLKS = PROJ_COLS // COL_BLK
N_PAIRS = 6
N_CLASSES = N_GROUPS * N_PAIRS
HALF = D // 2
XS_COLS = HALF + LANES

TM_IN = 1024
TQ = 256
TM_MIX = 2048
MIX_BLK = 256
TM_POST = 256
TM_DISP = 512
TM_MOE = 256
TM_COMB = 256

_PAIRS = [(a, b) for a in range(E_PER_GROUP) for b in range(a + 1, E_PER_GROUP)]
_CLASS_EA = np.array([g * E_PER_GROUP + _PAIRS[p][0] for g in range(N_GROUPS) for p in range(N_PAIRS)], np.int32)
_CLASS_EB = np.array([g * E_PER_GROUP + _PAIRS[p][1] for g in range(N_GROUPS) for p in range(N_PAIRS)], np.int32)


def _cparams(sem, vmem_mb):
    return pltpu.CompilerParams(dimension_semantics=sem, vmem_limit_bytes=vmem_mb * 1024 * 1024)


def _dot(a, b):
    return jnp.dot(a, b, preferred_element_type=F32)


def _dot_nt(a, b):
    return lax.dot_general(a, b, (((1,), (1,)), ((), ())), preferred_element_type=F32)


def _pack_halves(xb):
    hi = lax.bitcast_convert_type(xb[:, :HALF].astype(F32), U32)
    lo = lax.bitcast_convert_type(xb[:, HALF:].astype(F32), U32)
    return hi | (lo >> 16)


def _unpack_halves(words):
    hi = lax.bitcast_convert_type(words & jnp.uint32(0xFFFF0000), F32)
    lo = lax.bitcast_convert_type(words << 16, F32)
    return hi, lo


def _ada_kernel(c_ref, w_ref, b_ref, o_ref):
    c = c_ref[...]
    a = (c * jax.nn.sigmoid(c)).astype(BF16)
    o_ref[0] = _dot(a, w_ref[0].astype(BF16)) + b_ref[0]


def _ada_call(cond8, w_ada, b_ada3):
    L = w_ada.shape[0]
    tn = 1024
    return pl.pallas_call(
        _ada_kernel,
        grid=(L, N_MOD * D // tn),
        in_specs=[
            pl.BlockSpec((8, D), lambda l, j: (0, 0)),
            pl.BlockSpec((1, D, tn), lambda l, j: (l, 0, j)),
            pl.BlockSpec((1, 1, tn), lambda l, j: (l, 0, j)),
        ],
        out_specs=pl.BlockSpec((1, 8, tn), lambda l, j: (l, 0, j)),
        out_shape=jax.ShapeDtypeStruct((L, 8, N_MOD * D), F32),
        compiler_params=_cparams(("arbitrary", "arbitrary"), 40),
        name="ada_mod",
    )(cond8, w_ada, b_ada3)


def _split_bf16(x):
    hi = x.astype(BF16)
    lo = (x - hi.astype(F32)).astype(BF16)
    return hi, lo


def _group_rsqrt(acc, ones_ref):
    hi, lo = _split_bf16(acc * acc)
    blk = ones_ref.shape[0]
    parts = []
    for b in range(acc.shape[1] // blk):
        sl = slice(b * blk, (b + 1) * blk)
        parts.append(_dot(hi[:, sl], ones_ref[...]) + _dot(lo[:, sl], ones_ref[...]))
    return [lax.rsqrt(p * (1.0 / A_DH) + EPS) for p in parts]


def _swap16(x):
    lane = lax.broadcasted_iota(I32, x.shape, 1)
    up = pltpu.roll(x, LANES - ROPE_FREQ, 1)
    dn = pltpu.roll(x, ROPE_FREQ, 1)
    return jnp.where((lane & ROPE_FREQ) == 0, up, dn)


def _inproj_kernel(*refs, n_ctx_tiles, aliased, layer):
    if aliased:
        (x_ref, mod_ref, g1_ref, w_ref, tab_ref, ones_ref, _, _,
         proj_ref, kc_ref, vc_ref, h_ref, kv_buf, kv_sem) = refs
    else:
        (x_ref, mod_ref, g1_ref, w_ref, tab_ref, ones_ref,
         proj_ref, kc_ref, vc_ref, h_ref, kv_buf, kv_sem) = refs
    i = pl.program_id(0)
    j = pl.program_id(1)
    spt = kv_buf.shape[0]
    is_ctx = i < n_ctx_tiles

    def matmul():
        return _dot(h_ref[...], w_ref[0])

    def cache_copy(dst_ref):
        return pltpu.make_async_copy(kv_buf, dst_ref.at[pl.ds(i * spt, spt), layer], kv_sem)

    def qk_epilogue(acc, t0):
        rs = _group_rsqrt(acc, ones_ref)
        heads_per_blk = ones_ref.shape[0] // LANES
        out = []
        for h in range(A_HEADS):
            sl = slice(h * LANES, (h + 1) * LANES)
            x = acc[:, sl]
            r = rs[h // heads_per_blk][:, (h % heads_per_blk) * LANES:(h % heads_per_blk + 1) * LANES]
            out.append((x * tab_ref[t0] + _swap16(x) * tab_ref[t0 + 1]) * r)
        return out

    @pl.when(j == 0)
    def _():
        x = x_ref[...]
        ms = jnp.mean(x * x, axis=-1, keepdims=True)
        y = x * lax.rsqrt(ms + EPS) * g1_ref[...]
        h = y * (1.0 + mod_ref[0, 1:2, :]) + mod_ref[0, 0:1, :]
        h_ref[...] = h.astype(BF16)
        proj_ref[...] = jax.nn.sigmoid(matmul()).astype(BF16)

    @pl.when(jnp.logical_and(j > 0, j < N_GATE_BLKS))
    def _():
        proj_ref[...] = jax.nn.sigmoid(matmul()).astype(BF16)

    @pl.when(j == J_Q)
    def _():
        ys = qk_epilogue(matmul(), 0)
        for h in range(A_HEADS):
            proj_ref[:, h * LANES:(h + 1) * LANES] = ys[h].astype(BF16)

    @pl.when(j == J_K)
    def _():
        ys = qk_epilogue(matmul(), 2)
        for h in range(A_HEADS):
            proj_ref[:, h * LANES:(h + 1) * LANES] = ys[h].astype(BF16)

        @pl.when(is_ctx)
        def _():
            for h in range(A_HEADS):
                kv_buf[:, :, h * LANES:(h + 1) * LANES] = ys[h].reshape(spt, kv_buf.shape[1], LANES)
            cache_copy(kc_ref).start()

    @pl.when(j == J_V)
    def _():
        acc = matmul()
        proj_ref[...] = acc.astype(BF16)

        @pl.when(is_ctx)
        def _():
            cache_copy(kc_ref).wait()
            kv_buf[...] = acc.reshape(kv_buf.shape)
            cache_copy(vc_ref).start()

    @pl.when(j > J_V)
    def _():
        proj_ref[...] = matmul().astype(BF16)

        @pl.when(jnp.logical_and(is_ctx, j == J_V + 1))
        def _():
            cache_copy(vc_ref).wait()


def _group_ones():
    idx = np.arange(MXU_DIM) // A_DH
    return jnp.asarray((idx[:, None] == idx[None, :]).astype(np.float32), BF16)


def _inproj_call(l, x_all, mod_l, g1, w_in_b, tabs, kc, vc, cache_shape, dims):
    NC, NL, S, DS = dims
    M = NC + NL
    tm = TM_IN
    nct = NC // tm
    tpl = DS // tm
    seq_per_tile = tm // S
    aliased = kc is not None
    ones = _group_ones()

    def modrow(i):
        return jnp.where(i < nct, 0, 1 + (i - nct) // tpl)

    def roperow(i):
        return jnp.where(i < nct, 0, 1 + (i - nct) % tpl)

    def wcol(j):
        return jnp.where(j < N_GATE_BLKS, j + (W_IN_BLKS - N_GATE_BLKS), j - N_GATE_BLKS)

    in_specs = [
        pl.BlockSpec((tm, D), lambda i, j: (i, 0)),
        pl.BlockSpec((1, N_MOD, D), lambda i, j: (modrow(i), 0, 0)),
        pl.BlockSpec((1, D), lambda i, j: (0, 0)),
        pl.BlockSpec((1, D, COL_BLK), lambda i, j: (l, 0, wcol(j))),
        pl.BlockSpec((4, tm, LANES), lambda i, j: (0, roperow(i), 0)),
        pl.BlockSpec((MXU_DIM, MXU_DIM), lambda i, j: (0, 0)),
    ]
    args = [x_all, mod_l, g1, w_in_b, tabs, ones]
    aliases = {}
    if aliased:
        in_specs += [pl.BlockSpec(memory_space=pl.ANY), pl.BlockSpec(memory_space=pl.ANY)]
        args += [kc, vc]
        aliases = {6: 1, 7: 2}
    kern = functools.partial(_inproj_kernel, n_ctx_tiles=nct, aliased=aliased, layer=l)
    return pl.pallas_call(
        kern,
        grid=(M // tm, PROJ_COLS // COL_BLK),
        in_specs=in_specs,
        out_specs=[
            pl.BlockSpec((tm, COL_BLK), lambda i, j: (i, j)),
            pl.BlockSpec(memory_space=pl.ANY),
            pl.BlockSpec(memory_space=pl.ANY),
        ],
        out_shape=[
            jax.ShapeDtypeStruct((M, PROJ_COLS), BF16),
            jax.ShapeDtypeStruct(cache_shape, F32),
            jax.ShapeDtypeStruct(cache_shape, F32),
        ],
        scratch_shapes=[pltpu.VMEM((tm, D), BF16), pltpu.VMEM((seq_per_tile, S, A_WIDTH), F32),
                        pltpu.SemaphoreType.DMA(())],
        input_output_aliases=aliases,
        compiler_params=_cparams(("arbitrary", "arbitrary"), 56),
        name="in_proj",
    )(*args)


def _lambda(lq1, lk1, lq2, lk2, lam_init):
    a = jnp.sum(lq1[...] * lk1[...], axis=-1, keepdims=True)
    b = jnp.sum(lq2[...] * lk2[...], axis=-1, keepdims=True)
    return jnp.exp(a) - jnp.exp(b) + lam_init


def _attn_head(q_h, ks, vs, lam, sg, lam_init):
    lane = lax.broadcasted_iota(I32, q_h.shape, 1)
    lo = lane < A_DH
    zero = jnp.zeros_like(q_h)
    q0 = jnp.where(lo, q_h, zero)
    q1 = jnp.where(lo, zero, q_h)

    def soft(qm):
        ss = [_dot_nt(qm, k) for k in ks]
        m = functools.reduce(jnp.maximum, [jnp.max(s, axis=-1, keepdims=True) for s in ss])
        es = [jnp.exp2(s - m) for s in ss]
        den = functools.reduce(lambda a, b: a + b, [jnp.sum(e, axis=-1, keepdims=True) for e in es])
        return es, den

    e0, l0 = soft(q0)
    e1, l1 = soft(q1)
    r0 = 1.0 / l0
    r1 = lam / l1
    o = None
    for a, b, v in zip(e0, e1, vs):
        part = _dot((a * r0 - b * r1).astype(BF16), v)
        o = part if o is None else o + part
    ms = jnp.mean(o * o, axis=-1, keepdims=True)
    return (o * lax.rsqrt(ms + EPS) * sg) * (1.0 - lam_init)


def _attn_ctx_kernel(q_ref, k_ref, v_ref, lq1, lk1, lq2, lk2, sg_ref, o_ref, *, lam_init):
    lam = _lambda(lq1, lk1, lq2, lk2, lam_init)
    for h in range(A_HEADS):
        sl = slice(h * LANES, (h + 1) * LANES)
        o = _attn_head(q_ref[:, sl], [k_ref[:, sl]], [v_ref[:, sl]], lam, sg_ref[...], lam_init)
        o_ref[:, sl] = o.astype(BF16)


def _attn_lat_kernel(q_ref, k_ref, v_ref, ck_ref, cv_ref, lq1, lk1, lq2, lk2, sg_ref, oin_ref, o_ref,
                     *, lam_init):
    del oin_ref
    lam = _lambda(lq1, lk1, lq2, lk2, lam_init)
    for h in range(A_HEADS):
        sl = slice(h * LANES, (h + 1) * LANES)
        ks = [k_ref[:, sl], ck_ref[0, 0, :, sl].astype(BF16)]
        vs = [v_ref[:, sl], cv_ref[0, 0, :, sl].astype(BF16)]
        o = _attn_head(q_ref[:, sl], ks, vs, lam, sg_ref[...], lam_init)
        o_ref[:, sl] = o.astype(BF16)


def _attention_call(l, proj, cache_k4, cache_v4, lams, sg, lam_init, dims):
    NC, NL, S, DS = dims
    M = NC + NL
    B = NC // S
    DB = NL // DS
    lam_specs1 = [pl.BlockSpec((1, A_DH), lambda b: (0, 0))] * 4
    oa = pl.pallas_call(
        functools.partial(_attn_ctx_kernel, lam_init=lam_init),
        grid=(B,),
        in_specs=[
            pl.BlockSpec((S, A_WIDTH), lambda b: (b, J_Q)),
            pl.BlockSpec((S, A_WIDTH), lambda b: (b, J_K)),
            pl.BlockSpec((S, A_WIDTH), lambda b: (b, J_V)),
            *lam_specs1,
            pl.BlockSpec((1, LANES), lambda b: (0, 0)),
        ],
        out_specs=pl.BlockSpec((S, A_WIDTH), lambda b: (b, 0)),
        out_shape=jax.ShapeDtypeStruct((M, A_WIDTH), BF16),
        compiler_params=_cparams(("arbitrary",), 32),
        name="attn_ctx",
    )(proj, proj, proj, *lams, sg)

    nq = DS // TQ
    q0 = NC // TQ
    k0 = NC // DS
    P = cache_k4.shape[2]
    lam_specs2 = [pl.BlockSpec((1, A_DH), lambda b, qi: (0, 0))] * 4
    oa = pl.pallas_call(
        functools.partial(_attn_lat_kernel, lam_init=lam_init),
        grid=(DB, nq),
        in_specs=[
            pl.BlockSpec((TQ, A_WIDTH), lambda b, qi: (q0 + b * nq + qi, J_Q)),
            pl.BlockSpec((DS, A_WIDTH), lambda b, qi: (k0 + b, J_K)),
            pl.BlockSpec((DS, A_WIDTH), lambda b, qi: (k0 + b, J_V)),
            pl.BlockSpec((1, 1, P, A_WIDTH), lambda b, qi: (b, l, 0, 0)),
            pl.BlockSpec((1, 1, P, A_WIDTH), lambda b, qi: (b, l, 0, 0)),
            *lam_specs2,
            pl.BlockSpec((1, LANES), lambda b, qi: (0, 0)),
            pl.BlockSpec(memory_space=pl.ANY),
        ],
        out_specs=pl.BlockSpec((TQ, A_WIDTH), lambda b, qi: (q0 + b * nq + qi, 0)),
        out_shape=jax.ShapeDtypeStruct((M, A_WIDTH), BF16),
        input_output_aliases={10: 0},
        compiler_params=_cparams(("arbitrary", "arbitrary"), 48),
        name="attn_lat",
    )(proj, proj, proj, cache_k4, cache_v4, *lams, sg, oa)
    return oa


def _mix_kernel(pc_ref, bc_ref, wp_ref, ps_ref, cw_ref, cb_ref, o_ref, *, n_ctx_tiles, S, DS):
    i = pl.program_id(0)
    tm = pc_ref.shape[0]
    nb = tm // MIX_BLK
    is_lat = i >= n_ctx_tiles
    seqlen = jnp.where(is_lat, DS, S)
    rows = lax.broadcasted_iota(I32, (tm, 1), 0)
    pos = rows & (seqlen - 1)

    t = lax.broadcasted_iota(I32, (MIX_BLK, MIX_BLK), 0)
    s = lax.broadcasted_iota(I32, (MIX_BLK, MIX_BLK), 1)
    for g, win in enumerate(P_WINDOWS):
        half = win // 2
        sl = slice(g * P_GC, (g + 1) * P_GC)
        d_cur = s - t
        band_cur = jnp.logical_and(d_cur >= -half, d_cur < half).astype(BF16)
        d_prev = d_cur - MIX_BLK
        band_prev = jnp.logical_and(is_lat, jnp.logical_and(d_prev >= -half, d_prev < half)).astype(BF16)
        d_next = d_cur + MIX_BLK
        band_next = jnp.logical_and(is_lat, jnp.logical_and(d_next >= -half, d_next < half)).astype(BF16)
        for b in range(nb):
            r0 = b * MIX_BLK
            u_b = pc_ref[r0:r0 + MIX_BLK, sl]
            acc = _dot(band_cur, u_b)
            if b > 0:
                acc = acc + _dot(band_prev, pc_ref[r0 - MIX_BLK:r0, sl])
            if b + 1 < nb:
                acc = acc + _dot(band_next, pc_ref[r0 + MIX_BLK:r0 + 2 * MIX_BLK, sl])
            p = pos[r0:r0 + MIX_BLK]
            inv_cnt = 1.0 / (jnp.minimum(p + half, seqlen) - jnp.maximum(p - half, 0)).astype(F32)
            y = (acc * inv_cnt - u_b.astype(F32)).astype(BF16)
            o_ref[r0:r0 + MIX_BLK, sl] = (_dot(y, wp_ref[g]) * ps_ref[:, sl]).astype(BF16)

    u = pc_ref[:, P_WIDTH:].astype(F32)
    gate_b = bc_ref[:, :C_WIDTH].astype(F32)
    gate_c = bc_ref[:, C_WIDTH:].astype(F32)
    z = gate_c * u
    zm = jnp.where(pos >= 1, pltpu.roll(z, 1, 0), 0.0)
    zp = jnp.where(pos + 1 < seqlen, pltpu.roll(z, tm - 1, 0), 0.0)
    conv = zm * cw_ref[0:1, :] + z * cw_ref[1:2, :] + zp * cw_ref[2:3, :] + cb_ref[...]
    o_ref[:, P_WIDTH:] = (gate_b * conv).astype(BF16)


def _mix_call(proj, w_pool_l, pool_scale_l, conv_w_l, conv_b_l, dims):
    NC, NL, S, DS = dims
    M = NC + NL
    tm = TM_MIX
    assert S == MIX_BLK and tm == DS and NC % tm == 0 and max(P_WINDOWS) // 2 <= MIX_BLK
    kern = functools.partial(_mix_kernel, n_ctx_tiles=NC // tm, S=S, DS=DS)
    return pl.pallas_call(
        kern,
        grid=(M // tm,),
        in_specs=[
            pl.BlockSpec((tm, COL_BLK), lambda i: (i, J_PC)),
            pl.BlockSpec((tm, COL_BLK), lambda i: (i, J_BC)),
            pl.BlockSpec((P_GROUPS, P_GC, P_GC), lambda i: (0, 0, 0)),
            pl.BlockSpec((1, P_WIDTH), lambda i: (0, 0)),
            pl.BlockSpec((3, C_WIDTH), lambda i: (0, 0)),
            pl.BlockSpec((1, C_WIDTH), lambda i: (0, 0)),
        ],
        out_specs=pl.BlockSpec((tm, P_WIDTH + C_WIDTH), lambda i: (i, 0)),
        out_shape=jax.ShapeDtypeStruct((M, P_WIDTH + C_WIDTH), BF16),
        compiler_params=_cparams(("arbitrary",), 48),
        name="mixers",
    )(proj, proj, w_pool_l, pool_scale_l, conv_w_l, conv_b_l)


def _route(logits):
    lane = lax.broadcasted_iota(I32, logits.shape, 1)
    lane_f = lane.astype(F32)
    neg = -jnp.inf

    def first_argmax(v, vmax):
        return jnp.min(jnp.where(v == vmax, lane_f, float(LANES)), axis=-1, keepdims=True).astype(I32)

    gl = jnp.where(lane < N_GROUPS, logits, neg)
    gmax = jnp.max(gl, axis=-1, keepdims=True)
    gsel = first_argmax(gl, gmax)
    g_w = 1.0 / jnp.sum(jnp.exp(gl - gmax), axis=-1, keepdims=True)

    first = N_GROUPS + gsel * E_PER_GROUP
    in_grp = jnp.logical_and(lane >= first, lane < first + E_PER_GROUP)
    el = jnp.where(in_grp, logits, neg)
    t1 = jnp.max(el, axis=-1, keepdims=True)
    i1 = first_argmax(el, t1)
    el2 = jnp.where(lane == i1, neg, el)
    t2 = jnp.max(el2, axis=-1, keepdims=True)
    i2 = first_argmax(el2, t2)
    a = jnp.exp(t2 - t1)
    w1 = g_w / (1.0 + a)
    w2 = g_w * a / (1.0 + a)

    swap = i2 < i1
    llo = jnp.where(swap, i2, i1) - first
    lhi = jnp.where(swap, i1, i2) - first
    w_lo = jnp.where(swap, w2, w1)
    w_hi = jnp.where(swap, w1, w2)
    pair = ((llo * (7 - llo)) >> 1) + (lhi - llo - 1)
    cls = gsel * N_PAIRS + pair
    wvec = jnp.where(lane == 0, w_lo, jnp.where(lane == 1, w_hi, 0.0))
    return cls, wvec


def _post_kernel(x_ref, oa_ref, opc_ref, g0_ref, g1_ref, g2_ref, mod_ref, n2_ref, wa_ref, wp_ref, wc_ref,
                 wo_ref, wr_ref, br_ref, x1_ref, h2w_ref, cls_ref):
    a = _dot(oa_ref[...], wa_ref[0])
    merged = g0_ref[...].astype(F32) * a
    p = _dot(opc_ref[:, :P_WIDTH], wp_ref[0])
    merged = merged + g1_ref[...].astype(F32) * p
    c = _dot(opc_ref[:, P_WIDTH:], wc_ref[0])
    merged = merged + g2_ref[...].astype(F32) * c
    y = _dot(merged.astype(BF16), wo_ref[0])
    x1 = x_ref[...] + mod_ref[0, 2:3, :] * y
    x1_ref[...] = x1
    ms = jnp.mean(x1 * x1, axis=-1, keepdims=True)
    h2 = x1 * lax.rsqrt(ms + EPS) * n2_ref[...]
    h2 = h2 * (1.0 + mod_ref[0, 4:5, :]) + mod_ref[0, 3:4, :]
    h2b = h2.astype(BF16)
    h2w_ref[:, :HALF] = _pack_halves(h2b)
    logits = _dot(h2b, wr_ref[0]) + br_ref[0]
    cls, wvec = _route(logits)
    h2w_ref[:, HALF:] = lax.bitcast_convert_type(wvec, U32)
    cls_ref[...] = jnp.broadcast_to(cls, cls_ref.shape)


def _post_call(l, x_all, oa, opc, proj, mod_l, g2, wa, wp, wc, wo, wr, br, dims):
    NC, NL, S, DS = dims
    M = NC + NL
    tm = TM_POST
    nct = NC // tm
    tpl = DS // tm

    def modrow(i):
        return jnp.where(i < nct, 0, 1 + (i - nct) // tpl)

    const3 = lambda i: (l, 0, 0)
    single = pl.Buffered(1)
    return pl.pallas_call(
        _post_kernel,
        grid=(M // tm,),
        in_specs=[
            pl.BlockSpec((tm, D), lambda i: (i, 0)),
            pl.BlockSpec((tm, A_WIDTH), lambda i: (i, 0)),
            pl.BlockSpec((tm, P_WIDTH + C_WIDTH), lambda i: (i, 0)),
            pl.BlockSpec((tm, D), lambda i: (i, 0)),
            pl.BlockSpec((tm, D), lambda i: (i, 1)),
            pl.BlockSpec((tm, D), lambda i: (i, 2)),
            pl.BlockSpec((1, N_MOD, D), lambda i: (modrow(i), 0, 0)),
            pl.BlockSpec((1, D), lambda i: (0, 0)),
            pl.BlockSpec((1, A_WIDTH, D), const3, pipeline_mode=single),
            pl.BlockSpec((1, P_WIDTH, D), const3, pipeline_mode=single),
            pl.BlockSpec((1, C_WIDTH, D), const3, pipeline_mode=single),
            pl.BlockSpec((1, D, D), const3, pipeline_mode=single),
            pl.BlockSpec((1, D, LANES), const3, pipeline_mode=single),
            pl.BlockSpec((1, 1, LANES), const3),
        ],
        out_specs=[
            pl.BlockSpec((tm, D), lambda i: (i, 0)),
            pl.BlockSpec((tm, XS_COLS), lambda i: (i, 0)),
            pl.BlockSpec((tm, LANES), lambda i: (i, 0)),
        ],
        out_shape=[
            jax.ShapeDtypeStruct((M, D), F32),
            jax.ShapeDtypeStruct((M, XS_COLS), U32),
            jax.ShapeDtypeStruct((M, LANES), I32),
        ],
        compiler_params=_cparams(("arbitrary",), 52),
        name="post_attn",
    )(x_all, oa, opc, proj, proj, proj, mod_l, g2, wa, wp, wc, wo, wr, br)


def _sort_plan(cls, n_chunks):
    tm = TM_MOE
    onehot = (cls[:, None] == jnp.arange(N_CLASSES, dtype=I32)[None, :]).astype(I32)
    csum = jnp.cumsum(onehot, axis=0)
    rank = jnp.sum(onehot * csum, axis=1) - 1
    counts = csum[-1]
    padded = ((counts + tm - 1) // tm) * tm
    ends = jnp.cumsum(padded)
    starts = ends - padded
    pos = (jnp.sum(onehot * starts[None, :], axis=1) + rank).astype(I32)
    used = ends[-1] // tm
    chunk = jnp.arange(n_chunks, dtype=I32)
    cidx = jnp.minimum(chunk, used - 1)
    ccls = jnp.sum((ends[None, :] <= (cidx * tm)[:, None]).astype(I32), axis=1)
    ea = jnp.asarray(_CLASS_EA)[ccls]
    eb = jnp.asarray(_CLASS_EB)[ccls]
    valid = (chunk < used).astype(I32)
    return pos, cidx.astype(I32), ea, eb, valid


def _dispatch_kernel(pos_ref, h2w_ref, xs_in_ref, xs_ref, sem):
    del xs_in_ref
    tm = h2w_ref.shape[0]
    base = pl.program_id(0) * tm

    def issue(r, carry):
        dst = pos_ref[base + r]
        pltpu.make_async_copy(h2w_ref.at[pl.ds(r, 1)], xs_ref.at[pl.ds(dst, 1)], sem).start()
        return carry

    lax.fori_loop(0, tm, issue, 0, unroll=8)

    def drain(r, carry):
        pltpu.make_async_copy(h2w_ref.at[pl.ds(0, 1)], xs_ref.at[pl.ds(0, 1)], sem).wait()
        return carry

    lax.fori_loop(0, tm, drain, 0, unroll=8)


def _dispatch_call(pos, h2w, xs):
    M = h2w.shape[0]
    tm = TM_DISP
    return pl.pallas_call(
        _dispatch_kernel,
        grid_spec=pltpu.PrefetchScalarGridSpec(
            num_scalar_prefetch=1,
            grid=(M // tm,),
            in_specs=[
                pl.BlockSpec((tm, XS_COLS), lambda i, pos: (i, 0)),
                pl.BlockSpec(memory_space=pl.ANY),
            ],
            out_specs=pl.BlockSpec(memory_space=pl.ANY),
            scratch_shapes=[pltpu.SemaphoreType.DMA(())],
        ),
        out_shape=jax.ShapeDtypeStruct(xs.shape, U32),
        input_output_aliases={2: 0},
        compiler_params=_cparams(("arbitrary",), 32),
        name="dispatch",
    )(pos, h2w, xs)


def _moe_kernel(cidx_ref, ea_ref, eb_ref, valid_ref, xs_ref, wga, wua, wda, wgb, wub, wdb, ys_ref):
    del cidx_ref, ea_ref, eb_ref
    c = pl.program_id(0)

    @pl.when(valid_ref[c] == 1)
    def _():
        xa, xb = _unpack_halves(xs_ref[:, :HALF])
        x = jnp.concatenate([xa.astype(BF16), xb.astype(BF16)], axis=1)
        wts = lax.bitcast_convert_type(xs_ref[:, HALF:], F32)

        def expert(wg, wu, wd, w):
            g = _dot(x, wg[0, 0])
            u = _dot(x, wu[0, 0])
            act = (g * jax.nn.sigmoid(g)) * u
            return _dot((act * w).astype(BF16), wd[0, 0])

        y = expert(wga, wua, wda, wts[:, 0:1]) + expert(wgb, wub, wdb, wts[:, 1:2])
        ys_ref[...] = _pack_halves(y.astype(BF16))


def _moe_call(l, cidx, ea, eb, valid, xs, wg, wu, wd):
    n_chunks = cidx.shape[0]
    tm = TM_MOE
    up_a = pl.BlockSpec((1, 1, D, D_EXPERT), lambda c, ci, a, b, v: (l, a[c], 0, 0))
    up_b = pl.BlockSpec((1, 1, D, D_EXPERT), lambda c, ci, a, b, v: (l, b[c], 0, 0))
    dn_a = pl.BlockSpec((1, 1, D_EXPERT, D), lambda c, ci, a, b, v: (l, a[c], 0, 0))
    dn_b = pl.BlockSpec((1, 1, D_EXPERT, D), lambda c, ci, a, b, v: (l, b[c], 0, 0))
    return pl.pallas_call(
        _moe_kernel,
        grid_spec=pltpu.PrefetchScalarGridSpec(
            num_scalar_prefetch=4,
            grid=(n_chunks,),
            in_specs=[
                pl.BlockSpec((tm, XS_COLS), lambda c, ci, a, b, v: (ci[c], 0)),
                up_a, up_a, dn_a, up_b, up_b, dn_b,
            ],
            out_specs=pl.BlockSpec((tm, HALF), lambda c, ci, a, b, v: (ci[c], 0)),
        ),
        out_shape=jax.ShapeDtypeStruct((n_chunks * tm, HALF), U32),
        compiler_params=_cparams(("arbitrary",), 48),
        name="moe_experts",
    )(cidx, ea, eb, valid, xs, wg, wu, wd, wg, wu, wd)


def _combine_kernel(pos_ref, x1_ref, mod_ref, ys_ref, *rest, n_ctx_tiles, split):
    if split:
        op_ref, os_ref, ybuf, sem = rest
    else:
        o_ref, ybuf, sem = rest
    tm = x1_ref.shape[0]
    i = pl.program_id(0)
    n = pl.num_programs(0)

    def issue(step, slot):
        base = step * tm

        def body(r, carry):
            src = pos_ref[base + r]
            pltpu.make_async_copy(ys_ref.at[pl.ds(src, 1)], ybuf.at[slot, pl.ds(r, 1)], sem.at[slot]).start()
            return carry

        lax.fori_loop(0, tm, body, 0, unroll=8)

    @pl.when(i == 0)
    def _():
        issue(0, 0)

    @pl.when(i + 1 < n)
    def _():
        issue(i + 1, (i + 1) % 2)

    slot = i % 2

    def drain(r, carry):
        pltpu.make_async_copy(ys_ref.at[pl.ds(0, 1)], ybuf.at[slot, pl.ds(0, 1)], sem.at[slot]).wait()
        return carry

    lax.fori_loop(0, tm, drain, 0, unroll=8)
    ya, yb = _unpack_halves(ybuf[slot])
    out = x1_ref[...] + mod_ref[0, 5:6, :] * jnp.concatenate([ya, yb], axis=1)
    if split:
        @pl.when(i < n_ctx_tiles)
        def _():
            op_ref[...] = out

        @pl.when(i >= n_ctx_tiles)
        def _():
            os_ref[...] = out
    else:
        o_ref[...] = out


def _combine_call(pos, x1, mod_l, ys, dims, split):
    NC, NL, S, DS = dims
    M = NC + NL
    tm = TM_COMB
    nct = NC // tm
    tpl = DS // tm

    def modrow(i, pos):
        return (jnp.where(i < nct, 0, 1 + (i - nct) // tpl), 0, 0)

    if split:
        out_specs = [pl.BlockSpec((tm, D), lambda i, pos: (jnp.minimum(i, nct - 1), 0)),
                     pl.BlockSpec((tm, D), lambda i, pos: (jnp.maximum(i - nct, 0), 0))]
        out_shape = [jax.ShapeDtypeStruct((NC, D), F32), jax.ShapeDtypeStruct((NL, D), F32)]
    else:
        out_specs = pl.BlockSpec((tm, D), lambda i, pos: (i, 0))
        out_shape = jax.ShapeDtypeStruct((M, D), F32)
    return pl.pallas_call(
        functools.partial(_combine_kernel, n_ctx_tiles=nct, split=split),
        grid_spec=pltpu.PrefetchScalarGridSpec(
            num_scalar_prefetch=1,
            grid=(M // tm,),
            in_specs=[
                pl.BlockSpec((tm, D), lambda i, pos: (i, 0)),
                pl.BlockSpec((1, N_MOD, D), modrow),
                pl.BlockSpec(memory_space=pl.ANY),
            ],
            out_specs=out_specs,
            scratch_shapes=[pltpu.VMEM((2, tm, HALF), U32), pltpu.SemaphoreType.DMA((2,))],
        ),
        out_shape=out_shape,
        compiler_params=_cparams(("arbitrary",), 32),
        name="combine",
    )(pos, x1, mod_l, ys)


def _rope_tables(DS, tm):
    rows = DS // GRID_W
    pos_r = jnp.repeat(jnp.arange(rows, dtype=F32), GRID_W)
    pos_c = jnp.tile(jnp.arange(GRID_W, dtype=F32), rows)
    inv_freq = jnp.power(ROPE_THETA, -jnp.arange(ROPE_FREQ, dtype=F32) / ROPE_FREQ)
    ang = jnp.stack([pos_r[:, None] * inv_freq, pos_c[:, None] * inv_freq], axis=1)
    cos, sin = jnp.cos(ang), jnp.sin(ang)
    zeros = jnp.zeros_like(sin)
    c = jnp.tile(jnp.stack([cos, cos], axis=2).reshape(DS, A_DH), (1, 2))
    sm = jnp.tile(jnp.stack([-sin, zeros], axis=2).reshape(DS, A_DH), (1, 2))
    sp = jnp.tile(jnp.stack([zeros, sin], axis=2).reshape(DS, A_DH), (1, 2))
    ident = jnp.ones((tm, LANES), F32)
    zpad = jnp.zeros((tm, LANES), F32)
    return (jnp.concatenate([ident, c], 0), jnp.concatenate([zpad, sm], 0), jnp.concatenate([zpad, sp], 0))


def _qk_tables(rope, q_gain, k_gain):
    c, sm, sp = rope
    partner = np.arange(LANES) ^ ROPE_FREQ
    out = []
    for gain, scale in ((q_gain, A_DH ** -0.5 * math.log2(math.e)), (k_gain, 1.0)):
        g = jnp.tile(gain, 2) * scale
        out += [c * g, (sm + sp) * g[partner]]
    return jnp.stack(out, axis=0)


def kernel(x_prompt, x_sample, cache_k, cache_v, c, c_ctx, w_ada, b_ada, norm1_g, norm2_g, w_in, q_norm_g,
           k_norm_g, lam_q1, lam_k1, lam_q2, lam_k2, subln_g, w_pool, pool_scale, conv_w, conv_b, w_br_a,
           w_br_p, w_br_c, w_out, w_route_group, b_route_group, w_route_expert, b_route_expert, w_exp_gate,
           w_exp_up, w_exp_down):
    B, S, _ = x_prompt.shape
    DB, DS, _ = x_sample.shape
    L = w_in.shape[0]
    P = cache_k.shape[2]
    NC, NL = B * S, DB * DS
    M = NC + NL
    dims = (NC, NL, S, DS)
    assert DB + 1 <= 8 and NC % DS == 0 and NC % TM_IN == 0 and DS % TM_IN == 0 and TM_IN % S == 0
    assert S & (S - 1) == 0 and DS & (DS - 1) == 0

    cond8 = jnp.concatenate([c_ctx[None], c, jnp.zeros((8 - 1 - DB, D), F32)], axis=0)
    mod = _ada_call(cond8, w_ada, b_ada[:, None, :]).reshape(L, 8, N_MOD, D)

    w_in_b = w_in.astype(BF16)
    wa, wp, wc, wo = (w.astype(BF16) for w in (w_br_a, w_br_p, w_br_c, w_out))
    wg, wu, wd = (w.astype(BF16) for w in (w_exp_gate, w_exp_up, w_exp_down))
    wpool = w_pool.astype(BF16)
    wr = jnp.concatenate([w_route_group, w_route_expert,
                          jnp.zeros((L, D, LANES - N_GROUPS - N_EXPERTS), F32)], axis=2).astype(BF16)
    br = jnp.concatenate([b_route_group, b_route_expert,
                          jnp.zeros((L, LANES - N_GROUPS - N_EXPERTS), F32)], axis=1)[:, None, :]

    rope = _rope_tables(DS, TM_IN)
    cache_k4 = cache_k.reshape(DB, L, P, A_WIDTH)
    cache_v4 = cache_v.reshape(DB, L, P, A_WIDTH)

    x_all = jnp.concatenate([x_prompt.reshape(NC, D), x_sample.reshape(NL, D)], axis=0)
    kc = vc = None
    n_chunks = (M + N_CLASSES * (TM_MOE - 1)) // TM_MOE + 1
    xs = jnp.zeros((n_chunks * TM_MOE, XS_COLS), U32)

    for l in range(L):
        lam_init = 0.8 - 0.6 * math.exp(-0.3 * l)
        mod_l = mod[l]
        tabs = _qk_tables(rope, q_norm_g[l], k_norm_g[l])
        proj, kc, vc = _inproj_call(l, x_all, mod_l, norm1_g[l][None], w_in_b, tabs, kc, vc,
                                    (B, L, S, A_WIDTH), dims)
        lams = [a[l][None] for a in (lam_q1, lam_k1, lam_q2, lam_k2)]
        oa = _attention_call(l, proj, cache_k4, cache_v4, lams, subln_g[l][None], lam_init, dims)
        opc = _mix_call(proj, wpool[l], pool_scale[l][None], conv_w[l], conv_b[l][None], dims)
        x1, h2w, cls = _post_call(l, x_all, oa, opc, proj, mod_l, norm2_g[l][None],
                                  wa, wp, wc, wo, wr, br, dims)
        pos, cidx, ea, eb, valid = _sort_plan(cls[:, 0], n_chunks)
        xs = _dispatch_call(pos, h2w, xs)
        ys = _moe_call(l, cidx, ea, eb, valid, xs, wg, wu, wd)
        x_all = _combine_call(pos, x1, mod_l, ys, dims, split=(l == L - 1))

    y_p, y_s = x_all
    return (y_p.reshape(B, S, D), y_s.reshape(DB, DS, D),
            kc.reshape(B, L, S, A_HEADS, 2, A_DH), vc.reshape(B, L, S, A_HEADS, A_DV))
```

```python
import functools
import math

import numpy as np
import jax
import jax.numpy as jnp
from jax import lax
from jax.experimental import pallas as pl
from jax.experimental.pallas import tpu as pltpu

F32 = jnp.float32
BF16 = jnp.bfloat16
I32 = jnp.int32
U32 = jnp.uint32

D = 2048
N_MOD = 6
EPS = 1e-6
GRID_W = 64
A_WIDTH = 1024
A_HEADS = 8
A_DH = 64
A_DV = 128
ROPE_FREQ = 16
ROPE_THETA = 10000.0
P_WIDTH = 512
P_GROUPS = 4
P_GC = 128
P_WINDOWS = (2, 4, 8, 16)
C_WIDTH = 512
N_GROUPS = 4
E_PER_GROUP = 4
N_EXPERTS = 16
D_EXPERT = 512

LANES = 128
MXU_DIM = 256
N_DMA_PRIORITIES = 2
PROJ_COLS = 11264
COL_BLK = 1024
N_GATE_BLKS = 6
J_Q, J_K, J_V, J_PC, J_BC = 6, 7, 8, 9, 10
W_IN_BLKS = PROJ_COLS // COL_BLK
N_PAIRS = 6
N_CLASSES = N_GROUPS * N_PAIRS
HALF = D // 2
XS_COLS = HALF + LANES

TM_IN = 1024
TQ = 256
TM_MIX = 2048
MIX_BLK = 256
TM_POST = 256
TM_DISP = 512
TM_MOE = 256
TM_COMB = 256

_PAIRS = [(a, b) for a in range(E_PER_GROUP) for b in range(a + 1, E_PER_GROUP)]
_CLASS_EA = np.array([g * E_PER_GROUP + _PAIRS[p][0] for g in range(N_GROUPS) for p in range(N_PAIRS)], np.int32)
_CLASS_EB = np.array([g * E_PER_GROUP + _PAIRS[p][1] for g in range(N_GROUPS) for p in range(N_PAIRS)], np.int32)


def _cparams(sem, vmem_mb):
    return pltpu.CompilerParams(dimension_semantics=sem, vmem_limit_bytes=vmem_mb * 1024 * 1024)


def _dot(a, b):
    return jnp.dot(a, b, preferred_element_type=F32)


def _dot_nt(a, b):
    return lax.dot_general(a, b, (((1,), (1,)), ((), ())), preferred_element_type=F32)


def _sigmoid(x):
    return 0.5 * jnp.tanh(0.5 * x) + 0.5


def _pack_halves(xb):
    hi = lax.bitcast_convert_type(xb[:, :HALF].astype(F32), U32)
    lo = lax.bitcast_convert_type(xb[:, HALF:].astype(F32), U32)
    return hi | (lo >> 16)


def _unpack_halves(words):
    hi = lax.bitcast_convert_type(words & jnp.uint32(0xFFFF0000), F32)
    lo = lax.bitcast_convert_type(words << 16, F32)
    return hi, lo


def _ada_kernel(c_ref, w_ref, b_ref, o_ref):
    c = c_ref[...]
    a = (c * jax.nn.sigmoid(c)).astype(BF16)
    o_ref[0] = _dot(a, w_ref[0].astype(BF16)) + b_ref[0]


def _ada_call(cond8, w_ada, b_ada3):
    L = w_ada.shape[0]
    tn = 1024
    return pl.pallas_call(
        _ada_kernel,
        grid=(L, N_MOD * D // tn),
        in_specs=[
            pl.BlockSpec((8, D), lambda l, j: (0, 0)),
            pl.BlockSpec((1, D, tn), lambda l, j: (l, 0, j)),
            pl.BlockSpec((1, 1, tn), lambda l, j: (l, 0, j)),
        ],
        out_specs=pl.BlockSpec((1, 8, tn), lambda l, j: (l, 0, j)),
        out_shape=jax.ShapeDtypeStruct((L, 8, N_MOD * D), F32),
        compiler_params=_cparams(("arbitrary", "arbitrary"), 40),
        name="ada_mod",
    )(cond8, w_ada, b_ada3)


def _split_bf16(x):
    hi = x.astype(BF16)
    lo = (x - hi.astype(F32)).astype(BF16)
    return hi, lo


def _group_rsqrt(acc, ones_ref):
    hi, lo = _split_bf16(acc * acc)
    blk = ones_ref.shape[0]
    parts = []
    for b in range(acc.shape[1] // blk):
        sl = slice(b * blk, (b + 1) * blk)
        parts.append(_dot(hi[:, sl], ones_ref[...]) + _dot(lo[:, sl], ones_ref[...]))
    return [lax.rsqrt(p * (1.0 / A_DH) + EPS) for p in parts]


def _swap16(x):
    lane = lax.broadcasted_iota(I32, x.shape, 1)
    up = pltpu.roll(x, LANES - ROPE_FREQ, 1)
    dn = pltpu.roll(x, ROPE_FREQ, 1)
    return jnp.where((lane & ROPE_FREQ) == 0, up, dn)


def _inproj_kernel(*refs, n_ctx_tiles, aliased, layer):
    if aliased:
        (x_ref, mod_ref, g1_ref, w_ref, tab_ref, ones_ref, _, _,
         proj_ref, kc_ref, vc_ref, h_ref, kv_buf, kv_sem) = refs
    else:
        (x_ref, mod_ref, g1_ref, w_ref, tab_ref, ones_ref,
         proj_ref, kc_ref, vc_ref, h_ref, kv_buf, kv_sem) = refs
    i = pl.program_id(0)
    j = pl.program_id(1)
    spt = kv_buf.shape[0]
    is_ctx = i < n_ctx_tiles

    def matmul():
        return _dot(h_ref[...], w_ref[0])

    def cache_copy(dst_ref):
        return pltpu.make_async_copy(kv_buf, dst_ref.at[pl.ds(i * spt, spt), layer], kv_sem)

    def qk_epilogue(acc, t0):
        rs = _group_rsqrt(acc, ones_ref)
        heads_per_blk = ones_ref.shape[0] // LANES
        out = []
        for h in range(A_HEADS):
            sl = slice(h * LANES, (h + 1) * LANES)
            x = acc[:, sl]
            r = rs[h // heads_per_blk][:, (h % heads_per_blk) * LANES:(h % heads_per_blk + 1) * LANES]
            out.append((x * tab_ref[t0] + _swap16(x) * tab_ref[t0 + 1]) * r)
        return out

    @pl.when(j == 0)
    def _():
        x = x_ref[...]
        ms = jnp.mean(x * x, axis=-1, keepdims=True)
        y = x * lax.rsqrt(ms + EPS) * g1_ref[...]
        h = y * (1.0 + mod_ref[0, 1:2, :]) + mod_ref[0, 0:1, :]
        h_ref[...] = h.astype(BF16)
        proj_ref[...] = _sigmoid(matmul()).astype(BF16)

    @pl.when(jnp.logical_and(j > 0, j < N_GATE_BLKS))
    def _():
        proj_ref[...] = _sigmoid(matmul()).astype(BF16)

    @pl.when(j == J_Q)
    def _():
        ys = qk_epilogue(matmul(), 0)
        for h in range(A_HEADS):
            proj_ref[:, h * LANES:(h + 1) * LANES] = ys[h].astype(BF16)

    @pl.when(j == J_K)
    def _():
        ys = qk_epilogue(matmul(), 2)
        for h in range(A_HEADS):
            proj_ref[:, h * LANES:(h + 1) * LANES] = ys[h].astype(BF16)

        @pl.when(is_ctx)
        def _():
            for h in range(A_HEADS):
                kv_buf[:, :, h * LANES:(h + 1) * LANES] = ys[h].reshape(spt, kv_buf.shape[1], LANES)
            cache_copy(kc_ref).start()

    @pl.when(j == J_V)
    def _():
        acc = matmul()
        proj_ref[...] = acc.astype(BF16)

        @pl.when(is_ctx)
        def _():
            cache_copy(kc_ref).wait()
            kv_buf[...] = acc.reshape(kv_buf.shape)
            cache_copy(vc_ref).start()

    @pl.when(j > J_V)
    def _():
        proj_ref[...] = matmul().astype(BF16)

        @pl.when(jnp.logical_and(is_ctx, j == J_V + 1))
        def _():
            cache_copy(vc_ref).wait()


def _group_ones():
    idx = np.arange(MXU_DIM) // A_DH
    return jnp.asarray((idx[:, None] == idx[None, :]).astype(np.float32), BF16)


def _inproj_call(l, x_all, mod_l, g1, w_in_b, tabs, kc, vc, cache_shape, dims):
    NC, NL, S, DS = dims
    M = NC + NL
    tm = TM_IN
    nct = NC // tm
    tpl = DS // tm
    seq_per_tile = tm // S
    aliased = kc is not None
    ones = _group_ones()

    def modrow(i):
        return jnp.where(i < nct, 0, 1 + (i - nct) // tpl)

    def roperow(i):
        return jnp.where(i < nct, 0, 1 + (i - nct) % tpl)

    def wcol(j):
        return jnp.where(j < N_GATE_BLKS, j + (W_IN_BLKS - N_GATE_BLKS), j - N_GATE_BLKS)

    in_specs = [
        pl.BlockSpec((tm, D), lambda i, j: (i, 0)),
        pl.BlockSpec((1, N_MOD, D), lambda i, j: (modrow(i), 0, 0)),
        pl.BlockSpec((1, D), lambda i, j: (0, 0)),
        pl.BlockSpec((1, D, COL_BLK), lambda i, j: (l, 0, wcol(j))),
        pl.BlockSpec((4, tm, LANES), lambda i, j: (0, roperow(i), 0)),
        pl.BlockSpec((MXU_DIM, MXU_DIM), lambda i, j: (0, 0)),
    ]
    args = [x_all, mod_l, g1, w_in_b, tabs, ones]
    aliases = {}
    if aliased:
        in_specs += [pl.BlockSpec(memory_space=pl.ANY), pl.BlockSpec(memory_space=pl.ANY)]
        args += [kc, vc]
        aliases = {6: 1, 7: 2}
    kern = functools.partial(_inproj_kernel, n_ctx_tiles=nct, aliased=aliased, layer=l)
    return pl.pallas_call(
        kern,
        grid=(M // tm, PROJ_COLS // COL_BLK),
        in_specs=in_specs,
        out_specs=[
            pl.BlockSpec((tm, COL_BLK), lambda i, j: (i, j)),
            pl.BlockSpec(memory_space=pl.ANY),
            pl.BlockSpec(memory_space=pl.ANY),
        ],
        out_shape=[
            jax.ShapeDtypeStruct((M, PROJ_COLS), BF16),
            jax.ShapeDtypeStruct(cache_shape, F32),
            jax.ShapeDtypeStruct(cache_shape, F32),
        ],
        scratch_shapes=[pltpu.VMEM((tm, D), BF16), pltpu.VMEM((seq_per_tile, S, A_WIDTH), F32),
                        pltpu.SemaphoreType.DMA(())],
        input_output_aliases=aliases,
        compiler_params=_cparams(("arbitrary", "arbitrary"), 56),
        name="in_proj",
    )(*args)


def _lambda(lq1, lk1, lq2, lk2, lam_init):
    a = jnp.sum(lq1[...] * lk1[...], axis=-1, keepdims=True)
    b = jnp.sum(lq2[...] * lk2[...], axis=-1, keepdims=True)
    return jnp.exp(a) - jnp.exp(b) + lam_init


def _attn_head(q_h, ks, vs, lam, sg, lam_init):
    lane = lax.broadcasted_iota(I32, q_h.shape, 1)
    lo = lane < A_DH
    zero = jnp.zeros_like(q_h)
    q0 = jnp.where(lo, q_h, zero)
    q1 = jnp.where(lo, zero, q_h)

    def soft(qm):
        ss = [_dot_nt(qm, k) for k in ks]
        m = functools.reduce(jnp.maximum, [jnp.max(s, axis=-1, keepdims=True) for s in ss])
        es = [jnp.exp2(s - m) for s in ss]
        den = functools.reduce(lambda a, b: a + b, [jnp.sum(e, axis=-1, keepdims=True) for e in es])
        return es, den

    e0, l0 = soft(q0)
    e1, l1 = soft(q1)
    r0 = 1.0 / l0
    r1 = lam / l1
    o = None
    for a, b, v in zip(e0, e1, vs):
        part = _dot((a * r0 - b * r1).astype(BF16), v)
        o = part if o is None else o + part
    ms = jnp.mean(o * o, axis=-1, keepdims=True)
    return (o * lax.rsqrt(ms + EPS) * sg) * (1.0 - lam_init)


def _attn_ctx_kernel(q_ref, k_ref, v_ref, lq1, lk1, lq2, lk2, sg_ref, o_ref, *, lam_init):
    lam = _lambda(lq1, lk1, lq2, lk2, lam_init)
    for h in range(A_HEADS):
        sl = slice(h * LANES, (h + 1) * LANES)
        o = _attn_head(q_ref[:, sl], [k_ref[:, sl]], [v_ref[:, sl]], lam, sg_ref[...], lam_init)
        o_ref[:, sl] = o.astype(BF16)


def _attn_lat_kernel(q_ref, k_ref, v_ref, ck_ref, cv_ref, lq1, lk1, lq2, lk2, sg_ref, oin_ref, o_ref,
                     *, lam_init):
    del oin_ref
    lam = _lambda(lq1, lk1, lq2, lk2, lam_init)
    for h in range(A_HEADS):
        sl = slice(h * LANES, (h + 1) * LANES)
        ks = [k_ref[:, sl], ck_ref[0, 0, :, sl]]
        vs = [v_ref[:, sl], cv_ref[0, 0, :, sl]]
        o = _attn_head(q_ref[:, sl], ks, vs, lam, sg_ref[...], lam_init)
        o_ref[:, sl] = o.astype(BF16)


def _attention_call(l, proj, cache_k4, cache_v4, lams, sg, lam_init, dims):
    NC, NL, S, DS = dims
    M = NC + NL
    B = NC // S
    DB = NL // DS
    lam_specs1 = [pl.BlockSpec((1, A_DH), lambda b: (0, 0))] * 4
    oa = pl.pallas_call(
        functools.partial(_attn_ctx_kernel, lam_init=lam_init),
        grid=(B,),
        in_specs=[
            pl.BlockSpec((S, A_WIDTH), lambda b: (b, J_Q)),
            pl.BlockSpec((S, A_WIDTH), lambda b: (b, J_K)),
            pl.BlockSpec((S, A_WIDTH), lambda b: (b, J_V)),
            *lam_specs1,
            pl.BlockSpec((1, LANES), lambda b: (0, 0)),
        ],
        out_specs=pl.BlockSpec((S, A_WIDTH), lambda b: (b, 0)),
        out_shape=jax.ShapeDtypeStruct((M, A_WIDTH), BF16),
        compiler_params=_cparams(("arbitrary",), 32),
        name="attn_ctx",
    )(proj, proj, proj, *lams, sg)

    nq = DS // TQ
    q0 = NC // TQ
    k0 = NC // DS
    P = cache_k4.shape[2]
    lam_specs2 = [pl.BlockSpec((1, A_DH), lambda b, qi: (0, 0))] * 4
    oa = pl.pallas_call(
        functools.partial(_attn_lat_kernel, lam_init=lam_init),
        grid=(DB, nq),
        in_specs=[
            pl.BlockSpec((TQ, A_WIDTH), lambda b, qi: (q0 + b * nq + qi, J_Q)),
            pl.BlockSpec((DS, A_WIDTH), lambda b, qi: (k0 + b, J_K)),
            pl.BlockSpec((DS, A_WIDTH), lambda b, qi: (k0 + b, J_V)),
            pl.BlockSpec((1, 1, P, A_WIDTH), lambda b, qi: (b, l, 0, 0)),
            pl.BlockSpec((1, 1, P, A_WIDTH), lambda b, qi: (b, l, 0, 0)),
            *lam_specs2,
            pl.BlockSpec((1, LANES), lambda b, qi: (0, 0)),
            pl.BlockSpec(memory_space=pl.ANY),
        ],
        out_specs=pl.BlockSpec((TQ, A_WIDTH), lambda b, qi: (q0 + b * nq + qi, 0)),
        out_shape=jax.ShapeDtypeStruct((M, A_WIDTH), BF16),
        input_output_aliases={10: 0},
        compiler_params=_cparams(("arbitrary", "arbitrary"), 48),
        name="attn_lat",
    )(proj, proj, proj, cache_k4, cache_v4, *lams, sg, oa)
    return oa


def _mix_kernel(pc_ref, bc_ref, wp_ref, ps_ref, cw_ref, cb_ref, o_ref, *, n_ctx_tiles, S, DS):
    i = pl.program_id(0)
    tm = pc_ref.shape[0]
    nb = tm // MIX_BLK
    is_lat = i >= n_ctx_tiles
    seqlen = jnp.where(is_lat, DS, S)
    rows = lax.broadcasted_iota(I32, (tm, 1), 0)
    pos = rows & (seqlen - 1)

    t = lax.broadcasted_iota(I32, (MIX_BLK, MIX_BLK), 0)
    s = lax.broadcasted_iota(I32, (MIX_BLK, MIX_BLK), 1)
    for g, win in enumerate(P_WINDOWS):
        half = win // 2
        sl = slice(g * P_GC, (g + 1) * P_GC)
        d_cur = s - t
        band_cur = jnp.logical_and(d_cur >= -half, d_cur < half).astype(BF16)
        d_prev = d_cur - MIX_BLK
        band_prev = jnp.logical_and(is_lat, jnp.logical_and(d_prev >= -half, d_prev < half)).astype(BF16)
        d_next = d_cur + MIX_BLK
        band_next = jnp.logical_and(is_lat, jnp.logical_and(d_next >= -half, d_next < half)).astype(BF16)
        for b in range(nb):
            r0 = b * MIX_BLK
            u_b = pc_ref[r0:r0 + MIX_BLK, sl]
            acc = _dot(band_cur, u_b)
            if b > 0:
                acc = acc + _dot(band_prev, pc_ref[r0 - MIX_BLK:r0, sl])
            if b + 1 < nb:
                acc = acc + _dot(band_next, pc_ref[r0 + MIX_BLK:r0 + 2 * MIX_BLK, sl])
            p = pos[r0:r0 + MIX_BLK]
            inv_cnt = 1.0 / (jnp.minimum(p + half, seqlen) - jnp.maximum(p - half, 0)).astype(F32)
            y = (acc * inv_cnt - u_b.astype(F32)).astype(BF16)
            o_ref[r0:r0 + MIX_BLK, sl] = (_dot(y, wp_ref[g]) * ps_ref[:, sl]).astype(BF16)

    u = pc_ref[:, P_WIDTH:].astype(F32)
    gate_b = bc_ref[:, :C_WIDTH].astype(F32)
    gate_c = bc_ref[:, C_WIDTH:].astype(F32)
    z = gate_c * u
    zm = jnp.where(pos >= 1, pltpu.roll(z, 1, 0), 0.0)
    zp = jnp.where(pos + 1 < seqlen, pltpu.roll(z, tm - 1, 0), 0.0)
    conv = zm * cw_ref[0:1, :] + z * cw_ref[1:2, :] + zp * cw_ref[2:3, :] + cb_ref[...]
    o_ref[:, P_WIDTH:] = (gate_b * conv).astype(BF16)


def _mix_call(proj, w_pool_l, pool_scale_l, conv_w_l, conv_b_l, dims):
    NC, NL, S, DS = dims
    M = NC + NL
    tm = TM_MIX
    assert S == MIX_BLK and tm == DS and NC % tm == 0 and max(P_WINDOWS) // 2 <= MIX_BLK
    kern = functools.partial(_mix_kernel, n_ctx_tiles=NC // tm, S=S, DS=DS)
    return pl.pallas_call(
        kern,
        grid=(M // tm,),
        in_specs=[
            pl.BlockSpec((tm, COL_BLK), lambda i: (i, J_PC)),
            pl.BlockSpec((tm, COL_BLK), lambda i: (i, J_BC)),
            pl.BlockSpec((P_GROUPS, P_GC, P_GC), lambda i: (0, 0, 0)),
            pl.BlockSpec((1, P_WIDTH), lambda i: (0, 0)),
            pl.BlockSpec((3, C_WIDTH), lambda i: (0, 0)),
            pl.BlockSpec((1, C_WIDTH), lambda i: (0, 0)),
        ],
        out_specs=pl.BlockSpec((tm, P_WIDTH + C_WIDTH), lambda i: (i, 0)),
        out_shape=jax.ShapeDtypeStruct((M, P_WIDTH + C_WIDTH), BF16),
        compiler_params=_cparams(("arbitrary",), 48),
        name="mixers",
    )(proj, proj, w_pool_l, pool_scale_l, conv_w_l, conv_b_l)


def _route(logits):
    lane = lax.broadcasted_iota(I32, logits.shape, 1)
    lane_f = lane.astype(F32)
    neg = -jnp.inf

    def first_argmax(v, vmax):
        return jnp.min(jnp.where(v == vmax, lane_f, float(LANES)), axis=-1, keepdims=True).astype(I32)

    gl = jnp.where(lane < N_GROUPS, logits, neg)
    gmax = jnp.max(gl, axis=-1, keepdims=True)
    gsel = first_argmax(gl, gmax)
    g_w = 1.0 / jnp.sum(jnp.exp(gl - gmax), axis=-1, keepdims=True)

    first = N_GROUPS + gsel * E_PER_GROUP
    in_grp = jnp.logical_and(lane >= first, lane < first + E_PER_GROUP)
    el = jnp.where(in_grp, logits, neg)
    t1 = jnp.max(el, axis=-1, keepdims=True)
    i1 = first_argmax(el, t1)
    el2 = jnp.where(lane == i1, neg, el)
    t2 = jnp.max(el2, axis=-1, keepdims=True)
    i2 = first_argmax(el2, t2)
    a = jnp.exp(t2 - t1)
    w1 = g_w / (1.0 + a)
    w2 = g_w * a / (1.0 + a)

    swap = i2 < i1
    llo = jnp.where(swap, i2, i1) - first
    lhi = jnp.where(swap, i1, i2) - first
    w_lo = jnp.where(swap, w2, w1)
    w_hi = jnp.where(swap, w1, w2)
    pair = ((llo * (7 - llo)) >> 1) + (lhi - llo - 1)
    cls = gsel * N_PAIRS + pair
    wvec = jnp.where(lane == 0, w_lo, jnp.where(lane == 1, w_hi, 0.0))
    return cls, wvec


def _post_kernel(x_ref, oa_ref, opc_ref, g0_ref, g1_ref, g2_ref, mod_ref, n2_ref, wa_ref, wp_ref, wc_ref,
                 wo_ref, wr_ref, br_ref, x1_ref, h2w_ref, cls_ref):
    a = _dot(oa_ref[...], wa_ref[0])
    merged = g0_ref[...].astype(F32) * a
    p = _dot(opc_ref[:, :P_WIDTH], wp_ref[0])
    merged = merged + g1_ref[...].astype(F32) * p
    c = _dot(opc_ref[:, P_WIDTH:], wc_ref[0])
    merged = merged + g2_ref[...].astype(F32) * c
    y = _dot(merged.astype(BF16), wo_ref[0])
    x1 = x_ref[...] + mod_ref[0, 2:3, :] * y
    x1_ref[...] = x1
    ms = jnp.mean(x1 * x1, axis=-1, keepdims=True)
    h2 = x1 * lax.rsqrt(ms + EPS) * n2_ref[...]
    h2 = h2 * (1.0 + mod_ref[0, 4:5, :]) + mod_ref[0, 3:4, :]
    h2b = h2.astype(BF16)
    h2w_ref[:, :HALF] = _pack_halves(h2b)
    logits = _dot(h2b, wr_ref[0]) + br_ref[0]
    cls, wvec = _route(logits)
    h2w_ref[:, HALF:] = lax.bitcast_convert_type(wvec, U32)
    cls_ref[...] = jnp.broadcast_to(cls, cls_ref.shape)


def _post_call(l, x_all, oa, opc, proj, mod_l, g2, wa, wp, wc, wo, wr, br, dims):
    NC, NL, S, DS = dims
    M = NC + NL
    tm = TM_POST
    nct = NC // tm
    tpl = DS // tm

    def modrow(i):
        return jnp.where(i < nct, 0, 1 + (i - nct) // tpl)

    const3 = lambda i: (l, 0, 0)
    single = pl.Buffered(1)
    return pl.pallas_call(
        _post_kernel,
        grid=(M // tm,),
        in_specs=[
            pl.BlockSpec((tm, D), lambda i: (i, 0)),
            pl.BlockSpec((tm, A_WIDTH), lambda i: (i, 0)),
            pl.BlockSpec((tm, P_WIDTH + C_WIDTH), lambda i: (i, 0)),
            pl.BlockSpec((tm, D), lambda i: (i, 0)),
            pl.BlockSpec((tm, D), lambda i: (i, 1)),
            pl.BlockSpec((tm, D), lambda i: (i, 2)),
            pl.BlockSpec((1, N_MOD, D), lambda i: (modrow(i), 0, 0)),
            pl.BlockSpec((1, D), lambda i: (0, 0)),
            pl.BlockSpec((1, A_WIDTH, D), const3, pipeline_mode=single),
            pl.BlockSpec((1, P_WIDTH, D), const3, pipeline_mode=single),
            pl.BlockSpec((1, C_WIDTH, D), const3, pipeline_mode=single),
            pl.BlockSpec((1, D, D), const3, pipeline_mode=single),
            pl.BlockSpec((1, D, LANES), const3, pipeline_mode=single),
            pl.BlockSpec((1, 1, LANES), const3),
        ],
        out_specs=[
            pl.BlockSpec((tm, D), lambda i: (i, 0)),
            pl.BlockSpec((tm, XS_COLS), lambda i: (i, 0)),
            pl.BlockSpec((tm, LANES), lambda i: (i, 0)),
        ],
        out_shape=[
            jax.ShapeDtypeStruct((M, D), F32),
            jax.ShapeDtypeStruct((M, XS_COLS), U32),
            jax.ShapeDtypeStruct((M, LANES), I32),
        ],
        compiler_params=_cparams(("arbitrary",), 52),
        name="post_attn",
    )(x_all, oa, opc, proj, proj, proj, mod_l, g2, wa, wp, wc, wo, wr, br)


def _sort_plan(cls, n_chunks):
    tm = TM_MOE
    onehot = (cls[:, None] == jnp.arange(N_CLASSES, dtype=I32)[None, :]).astype(I32)
    csum = jnp.cumsum(onehot, axis=0)
    rank = jnp.sum(onehot * csum, axis=1) - 1
    counts = csum[-1]
    padded = ((counts + tm - 1) // tm) * tm
    ends = jnp.cumsum(padded)
    starts = ends - padded
    pos = (jnp.sum(onehot * starts[None, :], axis=1) + rank).astype(I32)
    used = ends[-1] // tm
    chunk = jnp.arange(n_chunks, dtype=I32)
    cidx = jnp.minimum(chunk, used - 1)
    ccls = jnp.sum((ends[None, :] <= (cidx * tm)[:, None]).astype(I32), axis=1)
    ea = jnp.asarray(_CLASS_EA)[ccls]
    eb = jnp.asarray(_CLASS_EB)[ccls]
    valid = (chunk < used).astype(I32)
    return pos, cidx.astype(I32), ea, eb, valid


def _dispatch_kernel(pos_ref, h2w_ref, xs_in_ref, xs_ref, sem):
    del xs_in_ref
    tm = h2w_ref.shape[0]
    base = pl.program_id(0) * tm

    def issue(p, carry):
        for prio in range(N_DMA_PRIORITIES):
            r = p * N_DMA_PRIORITIES + prio
            dst = pos_ref[base + r]
            pltpu.make_async_copy(h2w_ref.at[pl.ds(r, 1)], xs_ref.at[pl.ds(dst, 1)], sem).start(priority=prio)
        return carry

    lax.fori_loop(0, tm // N_DMA_PRIORITIES, issue, 0, unroll=4)

    def drain(r, carry):
        pltpu.make_async_copy(h2w_ref.at[pl.ds(0, 1)], xs_ref.at[pl.ds(0, 1)], sem).wait()
        return carry

    lax.fori_loop(0, tm, drain, 0, unroll=8)


def _dispatch_call(pos, h2w, xs):
    M = h2w.shape[0]
    tm = TM_DISP
    return pl.pallas_call(
        _dispatch_kernel,
        grid_spec=pltpu.PrefetchScalarGridSpec(
            num_scalar_prefetch=1,
            grid=(M // tm,),
            in_specs=[
                pl.BlockSpec((tm, XS_COLS), lambda i, pos: (i, 0)),
                pl.BlockSpec(memory_space=pl.ANY),
            ],
            out_specs=pl.BlockSpec(memory_space=pl.ANY),
            scratch_shapes=[pltpu.SemaphoreType.DMA(())],
        ),
        out_shape=jax.ShapeDtypeStruct(xs.shape, U32),
        input_output_aliases={2: 0},
        compiler_params=_cparams(("arbitrary",), 32),
        name="dispatch",
    )(pos, h2w, xs)


def _moe_kernel(cidx_ref, ea_ref, eb_ref, valid_ref, xs_ref, wga, wua, wda, wgb, wub, wdb, ys_ref):
    del cidx_ref, ea_ref, eb_ref
    c = pl.program_id(0)

    @pl.when(valid_ref[c] == 1)
    def _():
        xa, xb = _unpack_halves(xs_ref[:, :HALF])
        x = jnp.concatenate([xa.astype(BF16), xb.astype(BF16)], axis=1)
        wts = lax.bitcast_convert_type(xs_ref[:, HALF:], F32)

        def expert(wg, wu, wd, w):
            g = _dot(x, wg[0, 0])
            u = _dot(x, wu[0, 0])
            act = (g * jax.nn.sigmoid(g)) * u
            return _dot((act * w).astype(BF16), wd[0, 0])

        y = expert(wga, wua, wda, wts[:, 0:1]) + expert(wgb, wub, wdb, wts[:, 1:2])
        ys_ref[...] = _pack_halves(y.astype(BF16))


def _moe_call(l, cidx, ea, eb, valid, xs, wg, wu, wd):
    n_chunks = cidx.shape[0]
    tm = TM_MOE
    up_a = pl.BlockSpec((1, 1, D, D_EXPERT), lambda c, ci, a, b, v: (l, a[c], 0, 0))
    up_b = pl.BlockSpec((1, 1, D, D_EXPERT), lambda c, ci, a, b, v: (l, b[c], 0, 0))
    dn_a = pl.BlockSpec((1, 1, D_EXPERT, D), lambda c, ci, a, b, v: (l, a[c], 0, 0))
    dn_b = pl.BlockSpec((1, 1, D_EXPERT, D), lambda c, ci, a, b, v: (l, b[c], 0, 0))
    return pl.pallas_call(
        _moe_kernel,
        grid_spec=pltpu.PrefetchScalarGridSpec(
            num_scalar_prefetch=4,
            grid=(n_chunks,),
            in_specs=[
                pl.BlockSpec((tm, XS_COLS), lambda c, ci, a, b, v: (ci[c], 0)),
                up_a, up_a, dn_a, up_b, up_b, dn_b,
            ],
            out_specs=pl.BlockSpec((tm, HALF), lambda c, ci, a, b, v: (ci[c], 0)),
        ),
        out_shape=jax.ShapeDtypeStruct((n_chunks * tm, HALF), U32),
        compiler_params=_cparams(("arbitrary",), 48),
        name="moe_experts",
    )(cidx, ea, eb, valid, xs, wg, wu, wd, wg, wu, wd)


def _combine_kernel(pos_ref, x1_ref, mod_ref, ys_ref, *rest, n_ctx_tiles, split):
    if split:
        op_ref, os_ref, ybuf, sem = rest
    else:
        o_ref, ybuf, sem = rest
    tm = x1_ref.shape[0]
    i = pl.program_id(0)
    n = pl.num_programs(0)

    def issue(step, slot):
        base = step * tm

        def body(p, carry):
            for prio in range(N_DMA_PRIORITIES):
                r = p * N_DMA_PRIORITIES + prio
                src = pos_ref[base + r]
                pltpu.make_async_copy(ys_ref.at[pl.ds(src, 1)], ybuf.at[slot, pl.ds(r, 1)],
                                      sem.at[slot]).start(priority=prio)
            return carry

        lax.fori_loop(0, tm // N_DMA_PRIORITIES, body, 0, unroll=4)

    @pl.when(i == 0)
    def _():
        issue(0, 0)

    @pl.when(i + 1 < n)
    def _():
        issue(i + 1, (i + 1) % 2)

    slot = i % 2

    def drain(r, carry):
        pltpu.make_async_copy(ys_ref.at[pl.ds(0, 1)], ybuf.at[slot, pl.ds(0, 1)], sem.at[slot]).wait()
        return carry

    lax.fori_loop(0, tm, drain, 0, unroll=8)
    ya, yb = _unpack_halves(ybuf[slot])
    out = x1_ref[...] + mod_ref[0, 5:6, :] * jnp.concatenate([ya, yb], axis=1)
    if split:
        @pl.when(i < n_ctx_tiles)
        def _():
            op_ref[...] = out

        @pl.when(i >= n_ctx_tiles)
        def _():
            os_ref[...] = out
    else:
        o_ref[...] = out


def _combine_call(pos, x1, mod_l, ys, dims, split):
    NC, NL, S, DS = dims
    M = NC + NL
    tm = TM_COMB
    nct = NC // tm
    tpl = DS // tm

    def modrow(i, pos):
        return (jnp.where(i < nct, 0, 1 + (i - nct) // tpl), 0, 0)

    if split:
        out_specs = [pl.BlockSpec((tm, D), lambda i, pos: (jnp.minimum(i, nct - 1), 0)),
                     pl.BlockSpec((tm, D), lambda i, pos: (jnp.maximum(i - nct, 0), 0))]
        out_shape = [jax.ShapeDtypeStruct((NC, D), F32), jax.ShapeDtypeStruct((NL, D), F32)]
    else:
        out_specs = pl.BlockSpec((tm, D), lambda i, pos: (i, 0))
        out_shape = jax.ShapeDtypeStruct((M, D), F32)
    return pl.pallas_call(
        functools.partial(_combine_kernel, n_ctx_tiles=nct, split=split),
        grid_spec=pltpu.PrefetchScalarGridSpec(
            num_scalar_prefetch=1,
            grid=(M // tm,),
            in_specs=[
                pl.BlockSpec((tm, D), lambda i, pos: (i, 0)),
                pl.BlockSpec((1, N_MOD, D), modrow),
                pl.BlockSpec(memory_space=pl.ANY),
            ],
            out_specs=out_specs,
            scratch_shapes=[pltpu.VMEM((2, tm, HALF), U32), pltpu.SemaphoreType.DMA((2,))],
        ),
        out_shape=out_shape,
        compiler_params=_cparams(("arbitrary",), 32),
        name="combine",
    )(pos, x1, mod_l, ys)


def _rope_tables(DS, tm):
    rows = DS // GRID_W
    pos_r = jnp.repeat(jnp.arange(rows, dtype=F32), GRID_W)
    pos_c = jnp.tile(jnp.arange(GRID_W, dtype=F32), rows)
    inv_freq = jnp.power(ROPE_THETA, -jnp.arange(ROPE_FREQ, dtype=F32) / ROPE_FREQ)
    ang = jnp.stack([pos_r[:, None] * inv_freq, pos_c[:, None] * inv_freq], axis=1)
    cos, sin = jnp.cos(ang), jnp.sin(ang)
    zeros = jnp.zeros_like(sin)
    c = jnp.tile(jnp.stack([cos, cos], axis=2).reshape(DS, A_DH), (1, 2))
    sm = jnp.tile(jnp.stack([-sin, zeros], axis=2).reshape(DS, A_DH), (1, 2))
    sp = jnp.tile(jnp.stack([zeros, sin], axis=2).reshape(DS, A_DH), (1, 2))
    ident = jnp.ones((tm, LANES), F32)
    zpad = jnp.zeros((tm, LANES), F32)
    return (jnp.concatenate([ident, c], 0), jnp.concatenate([zpad, sm], 0), jnp.concatenate([zpad, sp], 0))


def _qk_tables(rope, q_gain, k_gain):
    c, sm, sp = rope
    partner = np.arange(LANES) ^ ROPE_FREQ
    out = []
    for gain, scale in ((q_gain, A_DH ** -0.5 * math.log2(math.e)), (k_gain, 1.0)):
        g = jnp.tile(gain, 2) * scale
        out += [c * g, (sm + sp) * g[partner]]
    return jnp.stack(out, axis=0)


def kernel(x_prompt, x_sample, cache_k, cache_v, c, c_ctx, w_ada, b_ada, norm1_g, norm2_g, w_in, q_norm_g,
           k_norm_g, lam_q1, lam_k1, lam_q2, lam_k2, subln_g, w_pool, pool_scale, conv_w, conv_b, w_br_a,
           w_br_p, w_br_c, w_out, w_route_group, b_route_group, w_route_expert, b_route_expert, w_exp_gate,
           w_exp_up, w_exp_down):
    B, S, _ = x_prompt.shape
    DB, DS, _ = x_sample.shape
    L = w_in.shape[0]
    P = cache_k.shape[2]
    NC, NL = B * S, DB * DS
    M = NC + NL
    dims = (NC, NL, S, DS)
    assert DB + 1 <= 8 and NC % DS == 0 and NC % TM_IN == 0 and DS % TM_IN == 0 and TM_IN % S == 0
    assert S & (S - 1) == 0 and DS & (DS - 1) == 0

    cond8 = jnp.concatenate([c_ctx[None], c, jnp.zeros((8 - 1 - DB, D), F32)], axis=0)
    mod = _ada_call(cond8, w_ada, b_ada[:, None, :]).reshape(L, 8, N_MOD, D)

    w_in_b = w_in.astype(BF16)
    wa, wp, wc, wo = (w.astype(BF16) for w in (w_br_a, w_br_p, w_br_c, w_out))
    wg, wu, wd = (w.astype(BF16) for w in (w_exp_gate, w_exp_up, w_exp_down))
    wpool = w_pool.astype(BF16)
    wr = jnp.concatenate([w_route_group, w_route_expert,
                          jnp.zeros((L, D, LANES - N_GROUPS - N_EXPERTS), F32)], axis=2).astype(BF16)
    br = jnp.concatenate([b_route_group, b_route_expert,
                          jnp.zeros((L, LANES - N_GROUPS - N_EXPERTS), F32)], axis=1)[:, None, :]

    rope = _rope_tables(DS, TM_IN)
    cache_k4 = cache_k.astype(BF16).reshape(DB, L, P, A_WIDTH)
    cache_v4 = cache_v.astype(BF16).reshape(DB, L, P, A_WIDTH)

    x_all = jnp.concatenate([x_prompt.reshape(NC, D), x_sample.reshape(NL, D)], axis=0)
    kc = vc = None
    n_chunks = (M + N_CLASSES * (TM_MOE - 1)) // TM_MOE + 1
    xs = jnp.zeros((n_chunks * TM_MOE, XS_COLS), U32)

    for l in range(L):
        lam_init = 0.8 - 0.6 * math.exp(-0.3 * l)
        mod_l = mod[l]
        tabs = _qk_tables(rope, q_norm_g[l], k_norm_g[l])
        proj, kc, vc = _inproj_call(l, x_all, mod_l, norm1_g[l][None], w_in_b, tabs, kc, vc,
                                    (B, L, S, A_WIDTH), dims)
        lams = [a[l][None] for a in (lam_q1, lam_k1, lam_q2, lam_k2)]
        oa = _attention_call(l, proj, cache_k4, cache_v4, lams, subln_g[l][None], lam_init, dims)
        opc = _mix_call(proj, wpool[l], pool_scale[l][None], conv_w[l], conv_b[l][None], dims)
        x1, h2w, cls = _post_call(l, x_all, oa, opc, proj, mod_l, norm2_g[l][None],
                                  wa, wp, wc, wo, wr, br, dims)
        pos, cidx, ea, eb, valid = _sort_plan(cls[:, 0], n_chunks)
        xs = _dispatch_call(pos, h2w, xs)
        ys = _moe_call(l, cidx, ea, eb, valid, xs, wg, wu, wd)
        x_all = _combine_call(pos, x1, mod_l, ys, dims, split=(l == L - 1))

    y_p, y_s = x_all
    return (y_p.reshape(B, S, D), y_s.reshape(DB, DS, D),
            kc.reshape(B, L, S, A_HEADS, 2, A_DH), vc.reshape(B, L, S, A_HEADS, A_DV))
```

```python
import functools
import math

import numpy as np
import jax
import jax.numpy as jnp
from jax import lax
from jax.experimental import pallas as pl
from jax.experimental.pallas import tpu as pltpu

F32 = jnp.float32
BF16 = jnp.bfloat16
I32 = jnp.int32
U32 = jnp.uint32

D = 2048
N_MOD = 6
EPS = 1e-6
GRID_W = 64
A_WIDTH = 1024
A_HEADS = 8
A_DH = 64
A_DV = 128
ROPE_FREQ = 16
ROPE_THETA = 10000.0
P_WIDTH = 512
P_GROUPS = 4
P_GC = 128
P_WINDOWS = (2, 4, 8, 16)
C_WIDTH = 512
N_GROUPS = 4
E_PER_GROUP = 4
N_EXPERTS = 16
D_EXPERT = 512

LANES = 128
MXU_DIM = 256
N_DMA_PRIORITIES = 2
PROJ_COLS = 11264
COL_BLK = 1024
N_GATE_BLKS = 6
J_Q, J_K, J_V, J_PC, J_BC = 6, 7, 8, 9, 10
W_IN_BLKS = PROJ_COLS // COL_BLK
N_PAIRS = 6
N_CLASSES = N_GROUPS * N_PAIRS
HALF = D // 2
XS_COLS = HALF + LANES

TM_IN = 1024
TQ = 256
TM_MIX = 2048
MIX_BLK = 256
TM_POST = 256
TM_DISP = 512
TM_MOE = 256
TM_COMB = 256

_PAIRS = [(a, b) for a in range(E_PER_GROUP) for b in range(a + 1, E_PER_GROUP)]
_CLASS_EA = np.array([g * E_PER_GROUP + _PAIRS[p][0] for g in range(N_GROUPS) for p in range(N_PAIRS)], np.int32)
_CLASS_EB = np.array([g * E_PER_GROUP + _PAIRS[p][1] for g in range(N_GROUPS) for p in range(N_PAIRS)], np.int32)


def _cparams(sem, vmem_mb):
    return pltpu.CompilerParams(dimension_semantics=sem, vmem_limit_bytes=vmem_mb * 1024 * 1024)


def _dot(a, b):
    return jnp.dot(a, b, preferred_element_type=F32)


def _dot_nt(a, b):
    return lax.dot_general(a, b, (((1,), (1,)), ((), ())), preferred_element_type=F32)


def _sigmoid(x):
    return 0.5 * jnp.tanh(0.5 * x) + 0.5


def _pack_halves(xb):
    hi = lax.bitcast_convert_type(xb[:, :HALF].astype(F32), U32)
    lo = lax.bitcast_convert_type(xb[:, HALF:].astype(F32), U32)
    return hi | (lo >> 16)


def _unpack_halves(words):
    hi = lax.bitcast_convert_type(words & jnp.uint32(0xFFFF0000), F32)
    lo = lax.bitcast_convert_type(words << 16, F32)
    return hi, lo


def _ada_kernel(c_ref, w_ref, b_ref, o_ref):
    c = c_ref[...]
    a = (c * jax.nn.sigmoid(c)).astype(BF16)
    o_ref[0] = _dot(a, w_ref[0].astype(BF16)) + b_ref[0]


def _ada_call(cond8, w_ada, b_ada3):
    L = w_ada.shape[0]
    tn = 1024
    return pl.pallas_call(
        _ada_kernel,
        grid=(L, N_MOD * D // tn),
        in_specs=[
            pl.BlockSpec((8, D), lambda l, j: (0, 0)),
            pl.BlockSpec((1, D, tn), lambda l, j: (l, 0, j)),
            pl.BlockSpec((1, 1, tn), lambda l, j: (l, 0, j)),
        ],
        out_specs=pl.BlockSpec((1, 8, tn), lambda l, j: (l, 0, j)),
        out_shape=jax.ShapeDtypeStruct((L, 8, N_MOD * D), F32),
        compiler_params=_cparams(("arbitrary", "arbitrary"), 40),
        name="ada_mod",
    )(cond8, w_ada, b_ada3)


def _split_bf16(x):
    hi = x.astype(BF16)
    lo = (x - hi.astype(F32)).astype(BF16)
    return hi, lo


def _group_rsqrt(acc, ones_ref):
    hi, lo = _split_bf16(acc * acc)
    blk = ones_ref.shape[0]
    parts = []
    for b in range(acc.shape[1] // blk):
        sl = slice(b * blk, (b + 1) * blk)
        parts.append(_dot(hi[:, sl], ones_ref[...]) + _dot(lo[:, sl], ones_ref[...]))
    return [lax.rsqrt(p * (1.0 / A_DH) + EPS) for p in parts]


def _swap16(x):
    lane = lax.broadcasted_iota(I32, x.shape, 1)
    up = pltpu.roll(x, LANES - ROPE_FREQ, 1)
    dn = pltpu.roll(x, ROPE_FREQ, 1)
    return jnp.where((lane & ROPE_FREQ) == 0, up, dn)


def _inproj_kernel(*refs, n_ctx_tiles, aliased, layer):
    if aliased:
        (x_ref, mod_ref, g1_ref, w_ref, tab_ref, ones_ref, _, _,
         proj_ref, kc_ref, vc_ref, h_ref, kv_buf, kv_sem) = refs
    else:
        (x_ref, mod_ref, g1_ref, w_ref, tab_ref, ones_ref,
         proj_ref, kc_ref, vc_ref, h_ref, kv_buf, kv_sem) = refs
    i = pl.program_id(0)
    j = pl.program_id(1)
    spt = kv_buf.shape[0]
    is_ctx = i < n_ctx_tiles

    def matmul():
        return _dot(h_ref[...], w_ref[0])

    def cache_copy(dst_ref):
        return pltpu.make_async_copy(kv_buf, dst_ref.at[pl.ds(i * spt, spt), layer], kv_sem)

    def qk_epilogue(acc, t0):
        rs = _group_rsqrt(acc, ones_ref)
        heads_per_blk = ones_ref.shape[0] // LANES
        out = []
        for h in range(A_HEADS):
            sl = slice(h * LANES, (h + 1) * LANES)
            x = acc[:, sl]
            r = rs[h // heads_per_blk][:, (h % heads_per_blk) * LANES:(h % heads_per_blk + 1) * LANES]
            out.append((x * tab_ref[t0] + _swap16(x) * tab_ref[t0 + 1]) * r)
        return out

    @pl.when(j == 0)
    def _():
        x = x_ref[...]
        ms = jnp.mean(x * x, axis=-1, keepdims=True)
        y = x * lax.rsqrt(ms + EPS) * g1_ref[...]
        h = y * (1.0 + mod_ref[0, 1:2, :]) + mod_ref[0, 0:1, :]
        h_ref[...] = h.astype(BF16)
        proj_ref[...] = _sigmoid(matmul()).astype(BF16)

    @pl.when(jnp.logical_and(j > 0, j < N_GATE_BLKS))
    def _():
        proj_ref[...] = _sigmoid(matmul()).astype(BF16)

    @pl.when(j == J_Q)
    def _():
        ys = qk_epilogue(matmul(), 0)
        for h in range(A_HEADS):
            proj_ref[:, h * LANES:(h + 1) * LANES] = ys[h].astype(BF16)

    @pl.when(j == J_K)
    def _():
        ys = qk_epilogue(matmul(), 2)
        for h in range(A_HEADS):
            proj_ref[:, h * LANES:(h + 1) * LANES] = ys[h].astype(BF16)

        @pl.when(is_ctx)
        def _():
            for h in range(A_HEADS):
                kv_buf[:, :, h * LANES:(h + 1) * LANES] = ys[h].reshape(spt, kv_buf.shape[1], LANES)
            cache_copy(kc_ref).start()

    @pl.when(j == J_V)
    def _():
        acc = matmul()
        proj_ref[...] = acc.astype(BF16)

        @pl.when(is_ctx)
        def _():
            cache_copy(kc_ref).wait()
            kv_buf[...] = acc.reshape(kv_buf.shape)
            cache_copy(vc_ref).start()

    @pl.when(j > J_V)
    def _():
        proj_ref[...] = matmul().astype(BF16)

        @pl.when(jnp.logical_and(is_ctx, j == J_V + 1))
        def _():
            cache_copy(vc_ref).wait()


def _group_ones():
    idx = np.arange(MXU_DIM) // A_DH
    return jnp.asarray((idx[:, None] == idx[None, :]).astype(np.float32), BF16)


def _inproj_call(l, x_all, mod_l, g1, w_in_b, tabs, kc, vc, cache_shape, dims):
    NC, NL, S, DS = dims
    M = NC + NL
    tm = TM_IN
    nct = NC // tm
    tpl = DS // tm
    seq_per_tile = tm // S
    aliased = kc is not None
    ones = _group_ones()

    def modrow(i):
        return jnp.where(i < nct, 0, 1 + (i - nct) // tpl)

    def roperow(i):
        return jnp.where(i < nct, 0, 1 + (i - nct) % tpl)

    def wcol(j):
        return jnp.where(j < N_GATE_BLKS, j + (W_IN_BLKS - N_GATE_BLKS), j - N_GATE_BLKS)

    in_specs = [
        pl.BlockSpec((tm, D), lambda i, j: (i, 0)),
        pl.BlockSpec((1, N_MOD, D), lambda i, j: (modrow(i), 0, 0)),
        pl.BlockSpec((1, D), lambda i, j: (0, 0)),
        pl.BlockSpec((1, D, COL_BLK), lambda i, j: (l, 0, wcol(j))),
        pl.BlockSpec((4, tm, LANES), lambda i, j: (0, roperow(i), 0)),
        pl.BlockSpec((MXU_DIM, MXU_DIM), lambda i, j: (0, 0)),
    ]
    args = [x_all, mod_l, g1, w_in_b, tabs, ones]
    aliases = {}
    if aliased:
        in_specs += [pl.BlockSpec(memory_space=pl.ANY), pl.BlockSpec(memory_space=pl.ANY)]
        args += [kc, vc]
        aliases = {6: 1, 7: 2}
    kern = functools.partial(_inproj_kernel, n_ctx_tiles=nct, aliased=aliased, layer=l)
    return pl.pallas_call(
        kern,
        grid=(M // tm, PROJ_COLS // COL_BLK),
        in_specs=in_specs,
        out_specs=[
            pl.BlockSpec((tm, COL_BLK), lambda i, j: (i, j)),
            pl.BlockSpec(memory_space=pl.ANY),
            pl.BlockSpec(memory_space=pl.ANY),
        ],
        out_shape=[
            jax.ShapeDtypeStruct((M, PROJ_COLS), BF16),
            jax.ShapeDtypeStruct(cache_shape, F32),
            jax.ShapeDtypeStruct(cache_shape, F32),
        ],
        scratch_shapes=[pltpu.VMEM((tm, D), BF16), pltpu.VMEM((seq_per_tile, S, A_WIDTH), F32),
                        pltpu.SemaphoreType.DMA(())],
        input_output_aliases=aliases,
        compiler_params=_cparams(("arbitrary", "arbitrary"), 56),
        name="in_proj",
    )(*args)


def _lambda(lq1, lk1, lq2, lk2, lam_init):
    a = jnp.sum(lq1[...] * lk1[...], axis=-1, keepdims=True)
    b = jnp.sum(lq2[...] * lk2[...], axis=-1, keepdims=True)
    return jnp.exp(a) - jnp.exp(b) + lam_init


def _attn_head(q_h, ks, vs, lam, sg, lam_init):
    lane = lax.broadcasted_iota(I32, q_h.shape, 1)
    lo = lane < A_DH
    zero = jnp.zeros_like(q_h)
    q0 = jnp.where(lo, q_h, zero)
    q1 = jnp.where(lo, zero, q_h)

    def soft(qm):
        ss = [_dot_nt(qm, k) for k in ks]
        m = functools.reduce(jnp.maximum, [jnp.max(s, axis=-1, keepdims=True) for s in ss])
        es = [jnp.exp2(s - m) for s in ss]
        den = functools.reduce(lambda a, b: a + b, [jnp.sum(e, axis=-1, keepdims=True) for e in es])
        return es, den

    e0, l0 = soft(q0)
    e1, l1 = soft(q1)
    r0 = 1.0 / l0
    r1 = lam / l1
    o = None
    for a, b, v in zip(e0, e1, vs):
        part = _dot((a * r0 - b * r1).astype(BF16), v)
        o = part if o is None else o + part
    ms = jnp.mean(o * o, axis=-1, keepdims=True)
    return (o * lax.rsqrt(ms + EPS) * sg) * (1.0 - lam_init)


def _attn_ctx_kernel(q_ref, k_ref, v_ref, lq1, lk1, lq2, lk2, sg_ref, o_ref, *, lam_init):
    lam = _lambda(lq1, lk1, lq2, lk2, lam_init)
    for h in range(A_HEADS):
        sl = slice(h * LANES, (h + 1) * LANES)
        o = _attn_head(q_ref[:, sl], [k_ref[:, sl]], [v_ref[:, sl]], lam, sg_ref[...], lam_init)
        o_ref[:, sl] = o.astype(BF16)


def _attn_lat_kernel(q_ref, k_ref, v_ref, ck_ref, cv_ref, lq1, lk1, lq2, lk2, sg_ref, oin_ref, o_ref,
                     *, lam_init):
    del oin_ref
    lam = _lambda(lq1, lk1, lq2, lk2, lam_init)
    for h in range(A_HEADS):
        sl = slice(h * LANES, (h + 1) * LANES)
        ks = [k_ref[:, sl], ck_ref[0, 0, :, sl]]
        vs = [v_ref[:, sl], cv_ref[0, 0, :, sl]]
        o = _attn_head(q_ref[:, sl], ks, vs, lam, sg_ref[...], lam_init)
        o_ref[:, sl] = o.astype(BF16)


def _attention_call(l, proj, cache_k4, cache_v4, lams, sg, lam_init, dims):
    NC, NL, S, DS = dims
    M = NC + NL
    B = NC // S
    DB = NL // DS
    lam_specs1 = [pl.BlockSpec((1, A_DH), lambda b: (0, 0))] * 4
    oa = pl.pallas_call(
        functools.partial(_attn_ctx_kernel, lam_init=lam_init),
        grid=(B,),
        in_specs=[
            pl.BlockSpec((S, A_WIDTH), lambda b: (b, J_Q)),
            pl.BlockSpec((S, A_WIDTH), lambda b: (b, J_K)),
            pl.BlockSpec((S, A_WIDTH), lambda b: (b, J_V)),
            *lam_specs1,
            pl.BlockSpec((1, LANES), lambda b: (0, 0)),
        ],
        out_specs=pl.BlockSpec((S, A_WIDTH), lambda b: (b, 0)),
        out_shape=jax.ShapeDtypeStruct((M, A_WIDTH), BF16),
        compiler_params=_cparams(("arbitrary",), 32),
        name="attn_ctx",
    )(proj, proj, proj, *lams, sg)

    nq = DS // TQ
    q0 = NC // TQ
    k0 = NC // DS
    P = cache_k4.shape[2]
    lam_specs2 = [pl.BlockSpec((1, A_DH), lambda b, qi: (0, 0))] * 4
    oa = pl.pallas_call(
        functools.partial(_attn_lat_kernel, lam_init=lam_init),
        grid=(DB, nq),
        in_specs=[
            pl.BlockSpec((TQ, A_WIDTH), lambda b, qi: (q0 + b * nq + qi, J_Q)),
            pl.BlockSpec((DS, A_WIDTH), lambda b, qi: (k0 + b, J_K)),
            pl.BlockSpec((DS, A_WIDTH), lambda b, qi: (k0 + b, J_V)),
            pl.BlockSpec((1, 1, P, A_WIDTH), lambda b, qi: (b, l, 0, 0)),
            pl.BlockSpec((1, 1, P, A_WIDTH), lambda b, qi: (b, l, 0, 0)),
            *lam_specs2,
            pl.BlockSpec((1, LANES), lambda b, qi: (0, 0)),
            pl.BlockSpec(memory_space=pl.ANY),
        ],
        out_specs=pl.BlockSpec((TQ, A_WIDTH), lambda b, qi: (q0 + b * nq + qi, 0)),
        out_shape=jax.ShapeDtypeStruct((M, A_WIDTH), BF16),
        input_output_aliases={10: 0},
        compiler_params=_cparams(("arbitrary", "arbitrary"), 48),
        name="attn_lat",
    )(proj, proj, proj, cache_k4, cache_v4, *lams, sg, oa)
    return oa


def _mix_kernel(pc_ref, bc_ref, wp_ref, ps_ref, cw_ref, cb_ref, o_ref, *, n_ctx_tiles, S, DS):
    i = pl.program_id(0)
    tm = pc_ref.shape[0]
    nb = tm // MIX_BLK
    is_lat = i >= n_ctx_tiles
    seqlen = jnp.where(is_lat, DS, S)
    rows = lax.broadcasted_iota(I32, (tm, 1), 0)
    pos = rows & (seqlen - 1)

    t = lax.broadcasted_iota(I32, (MIX_BLK, MIX_BLK), 0)
    s = lax.broadcasted_iota(I32, (MIX_BLK, MIX_BLK), 1)
    for g, win in enumerate(P_WINDOWS):
        half = win // 2
        sl = slice(g * P_GC, (g + 1) * P_GC)
        d_cur = s - t
        band_cur = jnp.logical_and(d_cur >= -half, d_cur < half).astype(BF16)
        d_prev = d_cur - MIX_BLK
        band_prev = jnp.logical_and(is_lat, jnp.logical_and(d_prev >= -half, d_prev < half)).astype(BF16)
        d_next = d_cur + MIX_BLK
        band_next = jnp.logical_and(is_lat, jnp.logical_and(d_next >= -half, d_next < half)).astype(BF16)
        for b in range(nb):
            r0 = b * MIX_BLK
            u_b = pc_ref[r0:r0 + MIX_BLK, sl]
            acc = _dot(band_cur, u_b)
            if b > 0:
                acc = acc + _dot(band_prev, pc_ref[r0 - MIX_BLK:r0, sl])
            if b + 1 < nb:
                acc = acc + _dot(band_next, pc_ref[r0 + MIX_BLK:r0 + 2 * MIX_BLK, sl])
            p = pos[r0:r0 + MIX_BLK]
            inv_cnt = 1.0 / (jnp.minimum(p + half, seqlen) - jnp.maximum(p - half, 0)).astype(F32)
            y = (acc * inv_cnt - u_b.astype(F32)).astype(BF16)
            o_ref[r0:r0 + MIX_BLK, sl] = (_dot(y, wp_ref[g]) * ps_ref[:, sl]).astype(BF16)

    u = pc_ref[:, P_WIDTH:].astype(F32)
    gate_b = bc_ref[:, :C_WIDTH].astype(F32)
    gate_c = bc_ref[:, C_WIDTH:].astype(F32)
    z = gate_c * u
    zm = jnp.where(pos >= 1, pltpu.roll(z, 1, 0), 0.0)
    zp = jnp.where(pos + 1 < seqlen, pltpu.roll(z, tm - 1, 0), 0.0)
    conv = zm * cw_ref[0:1, :] + z * cw_ref[1:2, :] + zp * cw_ref[2:3, :] + cb_ref[...]
    o_ref[:, P_WIDTH:] = (gate_b * conv).astype(BF16)


def _mix_call(proj, w_pool_l, pool_scale_l, conv_w_l, conv_b_l, dims):
    NC, NL, S, DS = dims
    M = NC + NL
    tm = TM_MIX
    assert S == MIX_BLK and tm == DS and NC % tm == 0 and max(P_WINDOWS) // 2 <= MIX_BLK
    kern = functools.partial(_mix_kernel, n_ctx_tiles=NC // tm, S=S, DS=DS)
    return pl.pallas_call(
        kern,
        grid=(M // tm,),
        in_specs=[
            pl.BlockSpec((tm, COL_BLK), lambda i: (i, J_PC)),
            pl.BlockSpec((tm, COL_BLK), lambda i: (i, J_BC)),
            pl.BlockSpec((P_GROUPS, P_GC, P_GC), lambda i: (0, 0, 0)),
            pl.BlockSpec((1, P_WIDTH), lambda i: (0, 0)),
            pl.BlockSpec((3, C_WIDTH), lambda i: (0, 0)),
            pl.BlockSpec((1, C_WIDTH), lambda i: (0, 0)),
        ],
        out_specs=pl.BlockSpec((tm, P_WIDTH + C_WIDTH), lambda i: (i, 0)),
        out_shape=jax.ShapeDtypeStruct((M, P_WIDTH + C_WIDTH), BF16),
        compiler_params=_cparams(("arbitrary",), 48),
        name="mixers",
    )(proj, proj, w_pool_l, pool_scale_l, conv_w_l, conv_b_l)


def _route(logits):
    lane = lax.broadcasted_iota(I32, logits.shape, 1)
    lane_f = lane.astype(F32)
    neg = -jnp.inf

    def first_argmax(v, vmax):
        return jnp.min(jnp.where(v == vmax, lane_f, float(LANES)), axis=-1, keepdims=True).astype(I32)

    gl = jnp.where(lane < N_GROUPS, logits, neg)
    gmax = jnp.max(gl, axis=-1, keepdims=True)
    gsel = first_argmax(gl, gmax)
    g_w = 1.0 / jnp.sum(jnp.exp(gl - gmax), axis=-1, keepdims=True)

    first = N_GROUPS + gsel * E_PER_GROUP
    in_grp = jnp.logical_and(lane >= first, lane < first + E_PER_GROUP)
    el = jnp.where(in_grp, logits, neg)
    t1 = jnp.max(el, axis=-1, keepdims=True)
    i1 = first_argmax(el, t1)
    el2 = jnp.where(lane == i1, neg, el)
    t2 = jnp.max(el2, axis=-1, keepdims=True)
    i2 = first_argmax(el2, t2)
    a = jnp.exp(t2 - t1)
    w1 = g_w / (1.0 + a)
    w2 = g_w * a / (1.0 + a)

    swap = i2 < i1
    llo = jnp.where(swap, i2, i1) - first
    lhi = jnp.where(swap, i1, i2) - first
    w_lo = jnp.where(swap, w2, w1)
    w_hi = jnp.where(swap, w1, w2)
    pair = ((llo * (7 - llo)) >> 1) + (lhi - llo - 1)
    cls = gsel * N_PAIRS + pair
    wvec = jnp.where(lane == 0, w_lo, jnp.where(lane == 1, w_hi, 0.0))
    return cls, wvec


def _post_kernel(x_ref, oa_ref, opc_ref, g0_ref, g1_ref, g2_ref, mod_ref, n2_ref, wa_ref, wp_ref, wc_ref,
                 wo_ref, wr_ref, br_ref, x1_ref, h2w_ref, cls_ref):
    a = _dot(oa_ref[...], wa_ref[0])
    merged = g0_ref[...].astype(F32) * a
    p = _dot(opc_ref[:, :P_WIDTH], wp_ref[0])
    merged = merged + g1_ref[...].astype(F32) * p
    c = _dot(opc_ref[:, P_WIDTH:], wc_ref[0])
    merged = merged + g2_ref[...].astype(F32) * c
    y = _dot(merged.astype(BF16), wo_ref[0])
    x1 = x_ref[...] + mod_ref[0, 2:3, :] * y
    x1_ref[...] = x1
    ms = jnp.mean(x1 * x1, axis=-1, keepdims=True)
    h2 = x1 * lax.rsqrt(ms + EPS) * n2_ref[...]
    h2 = h2 * (1.0 + mod_ref[0, 4:5, :]) + mod_ref[0, 3:4, :]
    h2b = h2.astype(BF16)
    h2w_ref[:, :HALF] = _pack_halves(h2b)
    logits = _dot(h2b, wr_ref[0]) + br_ref[0]
    cls, wvec = _route(logits)
    h2w_ref[:, HALF:] = lax.bitcast_convert_type(wvec, U32)
    cls_ref[...] = jnp.broadcast_to(cls, cls_ref.shape)


def _post_call(l, x_all, oa, opc, proj, mod_l, g2, wa, wp, wc, wo, wr, br, dims):
    NC, NL, S, DS = dims
    M = NC + NL
    tm = TM_POST
    nct = NC // tm
    tpl = DS // tm

    def modrow(i):
        return jnp.where(i < nct, 0, 1 + (i - nct) // tpl)

    const3 = lambda i: (l, 0, 0)
    single = pl.Buffered(1)
    return pl.pallas_call(
        _post_kernel,
        grid=(M // tm,),
        in_specs=[
            pl.BlockSpec((tm, D), lambda i: (i, 0)),
            pl.BlockSpec((tm, A_WIDTH), lambda i: (i, 0)),
            pl.BlockSpec((tm, P_WIDTH + C_WIDTH), lambda i: (i, 0)),
            pl.BlockSpec((tm, D), lambda i: (i, 0)),
            pl.BlockSpec((tm, D), lambda i: (i, 1)),
            pl.BlockSpec((tm, D), lambda i: (i, 2)),
            pl.BlockSpec((1, N_MOD, D), lambda i: (modrow(i), 0, 0)),
            pl.BlockSpec((1, D), lambda i: (0, 0)),
            pl.BlockSpec((1, A_WIDTH, D), const3, pipeline_mode=single),
            pl.BlockSpec((1, P_WIDTH, D), const3, pipeline_mode=single),
            pl.BlockSpec((1, C_WIDTH, D), const3, pipeline_mode=single),
            pl.BlockSpec((1, D, D), const3, pipeline_mode=single),
            pl.BlockSpec((1, D, LANES), const3, pipeline_mode=single),
            pl.BlockSpec((1, 1, LANES), const3),
        ],
        out_specs=[
            pl.BlockSpec((tm, D), lambda i: (i, 0)),
            pl.BlockSpec((tm, XS_COLS), lambda i: (i, 0)),
            pl.BlockSpec((tm, LANES), lambda i: (i, 0)),
        ],
        out_shape=[
            jax.ShapeDtypeStruct((M, D), F32),
            jax.ShapeDtypeStruct((M, XS_COLS), U32),
            jax.ShapeDtypeStruct((M, LANES), I32),
        ],
        compiler_params=_cparams(("arbitrary",), 52),
        name="post_attn",
    )(x_all, oa, opc, proj, proj, proj, mod_l, g2, wa, wp, wc, wo, wr, br)


def _sort_plan(cls, n_chunks):
    tm = TM_MOE
    onehot = (cls[:, None] == jnp.arange(N_CLASSES, dtype=I32)[None, :]).astype(I32)
    csum = jnp.cumsum(onehot, axis=0)
    rank = jnp.sum(onehot * csum, axis=1) - 1
    counts = csum[-1]
    padded = ((counts + tm - 1) // tm) * tm
    ends = jnp.cumsum(padded)
    starts = ends - padded
    pos = (jnp.sum(onehot * starts[None, :], axis=1) + rank).astype(I32)
    used = ends[-1] // tm
    chunk = jnp.arange(n_chunks, dtype=I32)
    cidx = jnp.minimum(chunk, used - 1)
    ccls = jnp.sum((ends[None, :] <= (cidx * tm)[:, None]).astype(I32), axis=1)
    ea = jnp.asarray(_CLASS_EA)[ccls]
    eb = jnp.asarray(_CLASS_EB)[ccls]
    valid = (chunk < used).astype(I32)
    return pos, cidx.astype(I32), ea, eb, valid


def _dispatch_kernel(pos_ref, h2w_ref, xs_in_ref, xs_ref, sem):
    del xs_in_ref
    tm = h2w_ref.shape[0]
    base = pl.program_id(0) * tm

    for r in range(tm):
        dst = pos_ref[base + r]
        pltpu.make_async_copy(h2w_ref.at[pl.ds(r, 1)], xs_ref.at[pl.ds(dst, 1)],
                              sem).start(priority=r % N_DMA_PRIORITIES)

    def drain(r, carry):
        pltpu.make_async_copy(h2w_ref.at[pl.ds(0, 1)], xs_ref.at[pl.ds(0, 1)], sem).wait()
        return carry

    lax.fori_loop(0, tm, drain, 0, unroll=8)


def _dispatch_call(pos, h2w, xs):
    M = h2w.shape[0]
    tm = TM_DISP
    return pl.pallas_call(
        _dispatch_kernel,
        grid_spec=pltpu.PrefetchScalarGridSpec(
            num_scalar_prefetch=1,
            grid=(M // tm,),
            in_specs=[
                pl.BlockSpec((tm, XS_COLS), lambda i, pos: (i, 0)),
                pl.BlockSpec(memory_space=pl.ANY),
            ],
            out_specs=pl.BlockSpec(memory_space=pl.ANY),
            scratch_shapes=[pltpu.SemaphoreType.DMA(())],
        ),
        out_shape=jax.ShapeDtypeStruct(xs.shape, U32),
        input_output_aliases={2: 0},
        compiler_params=_cparams(("arbitrary",), 32),
        name="dispatch",
    )(pos, h2w, xs)


def _moe_kernel(cidx_ref, ea_ref, eb_ref, valid_ref, xs_ref, wga, wua, wda, wgb, wub, wdb, ys_ref):
    del cidx_ref, ea_ref, eb_ref
    c = pl.program_id(0)

    @pl.when(valid_ref[c] == 1)
    def _():
        xa, xb = _unpack_halves(xs_ref[:, :HALF])
        x = jnp.concatenate([xa.astype(BF16), xb.astype(BF16)], axis=1)
        wts = lax.bitcast_convert_type(xs_ref[:, HALF:], F32)

        def expert(wg, wu, wd, w):
            g = _dot(x, wg[0, 0])
            u = _dot(x, wu[0, 0])
            act = (g * jax.nn.sigmoid(g)) * u
            return _dot((act * w).astype(BF16), wd[0, 0])

        y = expert(wga, wua, wda, wts[:, 0:1]) + expert(wgb, wub, wdb, wts[:, 1:2])
        ys_ref[...] = _pack_halves(y.astype(BF16))


def _moe_call(l, cidx, ea, eb, valid, xs, wg, wu, wd):
    n_chunks = cidx.shape[0]
    tm = TM_MOE
    up_a = pl.BlockSpec((1, 1, D, D_EXPERT), lambda c, ci, a, b, v: (l, a[c], 0, 0))
    up_b = pl.BlockSpec((1, 1, D, D_EXPERT), lambda c, ci, a, b, v: (l, b[c], 0, 0))
    dn_a = pl.BlockSpec((1, 1, D_EXPERT, D), lambda c, ci, a, b, v: (l, a[c], 0, 0))
    dn_b = pl.BlockSpec((1, 1, D_EXPERT, D), lambda c, ci, a, b, v: (l, b[c], 0, 0))
    return pl.pallas_call(
        _moe_kernel,
        grid_spec=pltpu.PrefetchScalarGridSpec(
            num_scalar_prefetch=4,
            grid=(n_chunks,),
            in_specs=[
                pl.BlockSpec((tm, XS_COLS), lambda c, ci, a, b, v: (ci[c], 0)),
                up_a, up_a, dn_a, up_b, up_b, dn_b,
            ],
            out_specs=pl.BlockSpec((tm, HALF), lambda c, ci, a, b, v: (ci[c], 0)),
        ),
        out_shape=jax.ShapeDtypeStruct((n_chunks * tm, HALF), U32),
        compiler_params=_cparams(("arbitrary",), 48),
        name="moe_experts",
    )(cidx, ea, eb, valid, xs, wg, wu, wd, wg, wu, wd)


def _combine_kernel(pos_ref, x1_ref, mod_ref, ys_ref, *rest, n_ctx_tiles, split):
    if split:
        op_ref, os_ref, ybuf, sem = rest
    else:
        o_ref, ybuf, sem = rest
    tm = x1_ref.shape[0]
    i = pl.program_id(0)
    n = pl.num_programs(0)

    def issue(step, slot):
        base = step * tm
        for r in range(tm):
            src = pos_ref[base + r]
            pltpu.make_async_copy(ys_ref.at[pl.ds(src, 1)], ybuf.at[slot, pl.ds(r, 1)],
                                  sem.at[slot]).start(priority=r % N_DMA_PRIORITIES)

    @pl.when(i == 0)
    def _():
        issue(0, 0)

    for nxt_slot in range(2):
        @pl.when(jnp.logical_and(i + 1 < n, (i + 1) % 2 == nxt_slot))
        def _():
            issue(i + 1, nxt_slot)

    slot = i % 2

    def drain(r, carry):
        pltpu.make_async_copy(ys_ref.at[pl.ds(0, 1)], ybuf.at[slot, pl.ds(0, 1)], sem.at[slot]).wait()
        return carry

    lax.fori_loop(0, tm, drain, 0, unroll=8)
    ya, yb = _unpack_halves(ybuf[slot])
    out = x1_ref[...] + mod_ref[0, 5:6, :] * jnp.concatenate([ya, yb], axis=1)
    if split:
        @pl.when(i < n_ctx_tiles)
        def _():
            op_ref[...] = out

        @pl.when(i >= n_ctx_tiles)
        def _():
            os_ref[...] = out
    else:
        o_ref[...] = out


def _combine_call(pos, x1, mod_l, ys, dims, split):
    NC, NL, S, DS = dims
    M = NC + NL
    tm = TM_COMB
    nct = NC // tm
    tpl = DS // tm

    def modrow(i, pos):
        return (jnp.where(i < nct, 0, 1 + (i - nct) // tpl), 0, 0)

    if split:
        out_specs = [pl.BlockSpec((tm, D), lambda i, pos: (jnp.minimum(i, nct - 1), 0)),
                     pl.BlockSpec((tm, D), lambda i, pos: (jnp.maximum(i - nct, 0), 0))]
        out_shape = [jax.ShapeDtypeStruct((NC, D), F32), jax.ShapeDtypeStruct((NL, D), F32)]
    else:
        out_specs = pl.BlockSpec((tm, D), lambda i, pos: (i, 0))
        out_shape = jax.ShapeDtypeStruct((M, D), F32)
    return pl.pallas_call(
        functools.partial(_combine_kernel, n_ctx_tiles=nct, split=split),
        grid_spec=pltpu.PrefetchScalarGridSpec(
            num_scalar_prefetch=1,
            grid=(M // tm,),
            in_specs=[
                pl.BlockSpec((tm, D), lambda i, pos: (i, 0)),
                pl.BlockSpec((1, N_MOD, D), modrow),
                pl.BlockSpec(memory_space=pl.ANY),
            ],
            out_specs=out_specs,
            scratch_shapes=[pltpu.VMEM((2, tm, HALF), U32), pltpu.SemaphoreType.DMA((2,))],
        ),
        out_shape=out_shape,
        compiler_params=_cparams(("arbitrary",), 32),
        name="combine",
    )(pos, x1, mod_l, ys)


def _rope_tables(DS, tm):
    rows = DS // GRID_W
    pos_r = jnp.repeat(jnp.arange(rows, dtype=F32), GRID_W)
    pos_c = jnp.tile(jnp.arange(GRID_W, dtype=F32), rows)
    inv_freq = jnp.power(ROPE_THETA, -jnp.arange(ROPE_FREQ, dtype=F32) / ROPE_FREQ)
    ang = jnp.stack([pos_r[:, None] * inv_freq, pos_c[:, None] * inv_freq], axis=1)
    cos, sin = jnp.cos(ang), jnp.sin(ang)
    zeros = jnp.zeros_like(sin)
    c = jnp.tile(jnp.stack([cos, cos], axis=2).reshape(DS, A_DH), (1, 2))
    sm = jnp.tile(jnp.stack([-sin, zeros], axis=2).reshape(DS, A_DH), (1, 2))
    sp = jnp.tile(jnp.stack([zeros, sin], axis=2).reshape(DS, A_DH), (1, 2))
    ident = jnp.ones((tm, LANES), F32)
    zpad = jnp.zeros((tm, LANES), F32)
    return (jnp.concatenate([ident, c], 0), jnp.concatenate([zpad, sm], 0), jnp.concatenate([zpad, sp], 0))


def _qk_tables(rope, q_gain, k_gain):
    c, sm, sp = rope
    partner = np.arange(LANES) ^ ROPE_FREQ
    out = []
    for gain, scale in ((q_gain, A_DH ** -0.5 * math.log2(math.e)), (k_gain, 1.0)):
        g = jnp.tile(gain, 2) * scale
        out += [c * g, (sm + sp) * g[partner]]
    return jnp.stack(out, axis=0)


def kernel(x_prompt, x_sample, cache_k, cache_v, c, c_ctx, w_ada, b_ada, norm1_g, norm2_g, w_in, q_norm_g,
           k_norm_g, lam_q1, lam_k1, lam_q2, lam_k2, subln_g, w_pool, pool_scale, conv_w, conv_b, w_br_a,
           w_br_p, w_br_c, w_out, w_route_group, b_route_group, w_route_expert, b_route_expert, w_exp_gate,
           w_exp_up, w_exp_down):
    B, S, _ = x_prompt.shape
    DB, DS, _ = x_sample.shape
    L = w_in.shape[0]
    P = cache_k.shape[2]
    NC, NL = B * S, DB * DS
    M = NC + NL
    dims = (NC, NL, S, DS)
    assert DB + 1 <= 8 and NC % DS == 0 and NC % TM_IN == 0 and DS % TM_IN == 0 and TM_IN % S == 0
    assert S & (S - 1) == 0 and DS & (DS - 1) == 0

    cond8 = jnp.concatenate([c_ctx[None], c, jnp.zeros((8 - 1 - DB, D), F32)], axis=0)
    mod = _ada_call(cond8, w_ada, b_ada[:, None, :]).reshape(L, 8, N_MOD, D)

    w_in_b = w_in.astype(BF16)
    wa, wp, wc, wo = (w.astype(BF16) for w in (w_br_a, w_br_p, w_br_c, w_out))
    wg, wu, wd = (w.astype(BF16) for w in (w_exp_gate, w_exp_up, w_exp_down))
    wpool = w_pool.astype(BF16)
    wr = jnp.concatenate([w_route_group, w_route_expert,
                          jnp.zeros((L, D, LANES - N_GROUPS - N_EXPERTS), F32)], axis=2).astype(BF16)
    br = jnp.concatenate([b_route_group, b_route_expert,
                          jnp.zeros((L, LANES - N_GROUPS - N_EXPERTS), F32)], axis=1)[:, None, :]

    rope = _rope_tables(DS, TM_IN)
    cache_k4 = cache_k.astype(BF16).reshape(DB, L, P, A_WIDTH)
    cache_v4 = cache_v.astype(BF16).reshape(DB, L, P, A_WIDTH)

    x_all = jnp.concatenate([x_prompt.reshape(NC, D), x_sample.reshape(NL, D)], axis=0)
    kc = vc = None
    n_chunks = (M + N_CLASSES * (TM_MOE - 1)) // TM_MOE + 1
    xs = jnp.zeros((n_chunks * TM_MOE, XS_COLS), U32)

    for l in range(L):
        lam_init = 0.8 - 0.6 * math.exp(-0.3 * l)
        mod_l = mod[l]
        tabs = _qk_tables(rope, q_norm_g[l], k_norm_g[l])
        proj, kc, vc = _inproj_call(l, x_all, mod_l, norm1_g[l][None], w_in_b, tabs, kc, vc,
                                    (B, L, S, A_WIDTH), dims)
        lams = [a[l][None] for a in (lam_q1, lam_k1, lam_q2, lam_k2)]
        oa = _attention_call(l, proj, cache_k4, cache_v4, lams, subln_g[l][None], lam_init, dims)
        opc = _mix_call(proj, wpool[l], pool_scale[l][None], conv_w[l], conv_b[l][None], dims)
        x1, h2w, cls = _post_call(l, x_all, oa, opc, proj, mod_l, norm2_g[l][None],
                                  wa, wp, wc, wo, wr, br, dims)
        pos, cidx, ea, eb, valid = _sort_plan(cls[:, 0], n_chunks)
        xs = _dispatch_call(pos, h2w, xs)
        ys = _moe_call(l, cidx, ea, eb, valid, xs, wg, wu, wd)
        x_all = _combine_call(pos, x1, mod_l, ys, dims, split=(l == L - 1))

    y_p, y_s = x_all
    return (y_p.reshape(B, S, D), y_s.reshape(DB, DS, D),
            kc.reshape(B, L, S, A_HEADS, 2, A_DH), vc.reshape(B, L, S, A_HEADS, A_DV))
```

```python
import functools
import math

import numpy as np
import jax
import jax.numpy as jnp
from jax import lax
from jax.experimental import pallas as pl
from jax.experimental.pallas import tpu as pltpu

F32 = jnp.float32
BF16 = jnp.bfloat16
I32 = jnp.int32
U32 = jnp.uint32

D = 2048
N_MOD = 6
EPS = 1e-6
GRID_W = 64
A_WIDTH = 1024
A_HEADS = 8
A_DH = 64
A_DV = 128
ROPE_FREQ = 16
ROPE_THETA = 10000.0
P_WIDTH = 512
P_GROUPS = 4
P_GC = 128
P_WINDOWS = (2, 4, 8, 16)
C_WIDTH = 512
N_GROUPS = 4
E_PER_GROUP = 4
N_EXPERTS = 16
D_EXPERT = 512

LANES = 128
MXU_DIM = 256
N_DMA_PRIORITIES = 2
PROJ_COLS = 11264
COL_BLK = 1024
N_GATE_BLKS = 6
J_Q, J_K, J_V, J_PC, J_BC = 6, 7, 8, 9, 10
W_IN_BLKS = PROJ_COLS // COL_BLK
N_PAIRS = 6
N_CLASSES = N_GROUPS * N_PAIRS
HALF = D // 2
XS_COLS = HALF + LANES

TM_IN = 1024
TQ = 256
TM_MIX = 2048
MIX_BLK = 256
TM_POST = 256
TM_DISP = 512
TM_MOE = 256
TM_COMB = 256

_PAIRS = [(a, b) for a in range(E_PER_GROUP) for b in range(a + 1, E_PER_GROUP)]
_CLASS_EA = np.array([g * E_PER_GROUP + _PAIRS[p][0] for g in range(N_GROUPS) for p in range(N_PAIRS)], np.int32)
_CLASS_EB = np.array([g * E_PER_GROUP + _PAIRS[p][1] for g in range(N_GROUPS) for p in range(N_PAIRS)], np.int32)


def _cparams(sem, vmem_mb):
    return pltpu.CompilerParams(dimension_semantics=sem, vmem_limit_bytes=vmem_mb * 1024 * 1024)


def _dot(a, b):
    return jnp.dot(a, b, preferred_element_type=F32)


def _dot_nt(a, b):
    return lax.dot_general(a, b, (((1,), (1,)), ((), ())), preferred_element_type=F32)


def _sigmoid(x):
    return 0.5 * jnp.tanh(0.5 * x) + 0.5


def _pack_halves(xb):
    hi = lax.bitcast_convert_type(xb[:, :HALF].astype(F32), U32)
    lo = lax.bitcast_convert_type(xb[:, HALF:].astype(F32), U32)
    return hi | (lo >> 16)


def _unpack_halves(words):
    hi = lax.bitcast_convert_type(words & jnp.uint32(0xFFFF0000), F32)
    lo = lax.bitcast_convert_type(words << 16, F32)
    return hi, lo


def _ada_kernel(c_ref, w_ref, b_ref, o_ref):
    c = c_ref[...]
    a = (c * jax.nn.sigmoid(c)).astype(BF16)
    o_ref[0] = _dot(a, w_ref[0].astype(BF16)) + b_ref[0]


def _ada_call(cond8, w_ada, b_ada3):
    L = w_ada.shape[0]
    tn = 1024
    return pl.pallas_call(
        _ada_kernel,
        grid=(L, N_MOD * D // tn),
        in_specs=[
            pl.BlockSpec((8, D), lambda l, j: (0, 0)),
            pl.BlockSpec((1, D, tn), lambda l, j: (l, 0, j)),
            pl.BlockSpec((1, 1, tn), lambda l, j: (l, 0, j)),
        ],
        out_specs=pl.BlockSpec((1, 8, tn), lambda l, j: (l, 0, j)),
        out_shape=jax.ShapeDtypeStruct((L, 8, N_MOD * D), F32),
        compiler_params=_cparams(("arbitrary", "arbitrary"), 40),
        name="ada_mod",
    )(cond8, w_ada, b_ada3)


def _split_bf16(x):
    hi = x.astype(BF16)
    lo = (x - hi.astype(F32)).astype(BF16)
    return hi, lo


def _group_rsqrt(acc, ones_ref):
    hi, lo = _split_bf16(acc * acc)
    blk = ones_ref.shape[0]
    parts = []
    for b in range(acc.shape[1] // blk):
        sl = slice(b * blk, (b + 1) * blk)
        parts.append(_dot(hi[:, sl], ones_ref[...]) + _dot(lo[:, sl], ones_ref[...]))
    return [lax.rsqrt(p * (1.0 / A_DH) + EPS) for p in parts]


def _swap16(x):
    lane = lax.broadcasted_iota(I32, x.shape, 1)
    up = pltpu.roll(x, LANES - ROPE_FREQ, 1)
    dn = pltpu.roll(x, ROPE_FREQ, 1)
    return jnp.where((lane & ROPE_FREQ) == 0, up, dn)


def _inproj_kernel(*refs, n_ctx_tiles, aliased, layer):
    if aliased:
        (x_ref, mod_ref, g1_ref, w_ref, tab_ref, ones_ref, _, _,
         proj_ref, kc_ref, vc_ref, h_ref, kv_buf, kv_sem) = refs
    else:
        (x_ref, mod_ref, g1_ref, w_ref, tab_ref, ones_ref,
         proj_ref, kc_ref, vc_ref, h_ref, kv_buf, kv_sem) = refs
    i = pl.program_id(0)
    j = pl.program_id(1)
    spt = kv_buf.shape[0]
    is_ctx = i < n_ctx_tiles

    def matmul():
        return _dot(h_ref[...], w_ref[0])

    def cache_copy(dst_ref):
        return pltpu.make_async_copy(kv_buf, dst_ref.at[pl.ds(i * spt, spt), layer], kv_sem)

    def qk_epilogue(acc, t0):
        rs = _group_rsqrt(acc, ones_ref)
        heads_per_blk = ones_ref.shape[0] // LANES
        out = []
        for h in range(A_HEADS):
            sl = slice(h * LANES, (h + 1) * LANES)
            x = acc[:, sl]
            r = rs[h // heads_per_blk][:, (h % heads_per_blk) * LANES:(h % heads_per_blk + 1) * LANES]
            out.append((x * tab_ref[t0] + _swap16(x) * tab_ref[t0 + 1]) * r)
        return out

    @pl.when(j == 0)
    def _():
        x = x_ref[...]
        ms = jnp.mean(x * x, axis=-1, keepdims=True)
        y = x * lax.rsqrt(ms + EPS) * g1_ref[...]
        h = y * (1.0 + mod_ref[0, 1:2, :]) + mod_ref[0, 0:1, :]
        h_ref[...] = h.astype(BF16)
        proj_ref[...] = _sigmoid(matmul()).astype(BF16)

    @pl.when(jnp.logical_and(j > 0, j < N_GATE_BLKS))
    def _():
        proj_ref[...] = _sigmoid(matmul()).astype(BF16)

    @pl.when(j == J_Q)
    def _():
        ys = qk_epilogue(matmul(), 0)
        for h in range(A_HEADS):
            proj_ref[:, h * LANES:(h + 1) * LANES] = ys[h].astype(BF16)

    @pl.when(j == J_K)
    def _():
        ys = qk_epilogue(matmul(), 2)
        for h in range(A_HEADS):
            proj_ref[:, h * LANES:(h + 1) * LANES] = ys[h].astype(BF16)

        @pl.when(is_ctx)
        def _():
            for h in range(A_HEADS):
                kv_buf[:, :, h * LANES:(h + 1) * LANES] = ys[h].reshape(spt, kv_buf.shape[1], LANES)
            cache_copy(kc_ref).start()

    @pl.when(j == J_V)
    def _():
        acc = matmul()
        proj_ref[...] = acc.astype(BF16)

        @pl.when(is_ctx)
        def _():
            cache_copy(kc_ref).wait()
            kv_buf[...] = acc.reshape(kv_buf.shape)
            cache_copy(vc_ref).start()

    @pl.when(j > J_V)
    def _():
        proj_ref[...] = matmul().astype(BF16)

        @pl.when(jnp.logical_and(is_ctx, j == J_V + 1))
        def _():
            cache_copy(vc_ref).wait()


def _group_ones():
    idx = np.arange(MXU_DIM) // A_DH
    return jnp.asarray((idx[:, None] == idx[None, :]).astype(np.float32), BF16)


def _inproj_call(l, x_all, mod_l, g1, w_in_b, tabs, kc, vc, cache_shape, dims):
    NC, NL, S, DS = dims
    M = NC + NL
    tm = TM_IN
    nct = NC // tm
    tpl = DS // tm
    seq_per_tile = tm // S
    aliased = kc is not None
    ones = _group_ones()

    def modrow(i):
        return jnp.where(i < nct, 0, 1 + (i - nct) // tpl)

    def roperow(i):
        return jnp.where(i < nct, 0, 1 + (i - nct) % tpl)

    def wcol(j):
        return jnp.where(j < N_GATE_BLKS, j + (W_IN_BLKS - N_GATE_BLKS), j - N_GATE_BLKS)

    in_specs = [
        pl.BlockSpec((tm, D), lambda i, j: (i, 0)),
        pl.BlockSpec((1, N_MOD, D), lambda i, j: (modrow(i), 0, 0)),
        pl.BlockSpec((1, D), lambda i, j: (0, 0)),
        pl.BlockSpec((1, D, COL_BLK), lambda i, j: (l, 0, wcol(j))),
        pl.BlockSpec((4, tm, LANES), lambda i, j: (0, roperow(i), 0)),
        pl.BlockSpec((MXU_DIM, MXU_DIM), lambda i, j: (0, 0)),
    ]
    args = [x_all, mod_l, g1, w_in_b, tabs, ones]
    aliases = {}
    if aliased:
        in_specs += [pl.BlockSpec(memory_space=pl.ANY), pl.BlockSpec(memory_space=pl.ANY)]
        args += [kc, vc]
        aliases = {6: 1, 7: 2}
    kern = functools.partial(_inproj_kernel, n_ctx_tiles=nct, aliased=aliased, layer=l)
    return pl.pallas_call(
        kern,
        grid=(M // tm, PROJ_COLS // COL_BLK),
        in_specs=in_specs,
        out_specs=[
            pl.BlockSpec((tm, COL_BLK), lambda i, j: (i, j)),
            pl.BlockSpec(memory_space=pl.ANY),
            pl.BlockSpec(memory_space=pl.ANY),
        ],
        out_shape=[
            jax.ShapeDtypeStruct((M, PROJ_COLS), BF16),
            jax.ShapeDtypeStruct(cache_shape, F32),
            jax.ShapeDtypeStruct(cache_shape, F32),
        ],
        scratch_shapes=[pltpu.VMEM((tm, D), BF16), pltpu.VMEM((seq_per_tile, S, A_WIDTH), F32),
                        pltpu.SemaphoreType.DMA(())],
        input_output_aliases=aliases,
        compiler_params=_cparams(("arbitrary", "arbitrary"), 56),
        name="in_proj",
    )(*args)


def _lambda(lq1, lk1, lq2, lk2, lam_init):
    a = jnp.sum(lq1[...] * lk1[...], axis=-1, keepdims=True)
    b = jnp.sum(lq2[...] * lk2[...], axis=-1, keepdims=True)
    return jnp.exp(a) - jnp.exp(b) + lam_init


def _head_scores(q_h, ks):
    lane = lax.broadcasted_iota(I32, q_h.shape, 1)
    lo = lane < A_DH
    zero = jnp.zeros_like(q_h)
    q0 = jnp.where(lo, q_h, zero)
    q1 = jnp.where(lo, zero, q_h)
    return [_dot_nt(q0, k) for k in ks], [_dot_nt(q1, k) for k in ks]


def _head_output(scores, vs, lam, sg, lam_init):
    def soft(ss):
        m = functools.reduce(jnp.maximum, [jnp.max(s, axis=-1, keepdims=True) for s in ss])
        es = [jnp.exp2(s - m) for s in ss]
        den = functools.reduce(lambda a, b: a + b, [jnp.sum(e, axis=-1, keepdims=True) for e in es])
        return es, den

    e0, l0 = soft(scores[0])
    e1, l1 = soft(scores[1])
    r0 = 1.0 / l0
    r1 = lam / l1
    o = None
    for a, b, v in zip(e0, e1, vs):
        part = _dot((a * r0 - b * r1).astype(BF16), v)
        o = part if o is None else o + part
    ms = jnp.mean(o * o, axis=-1, keepdims=True)
    return (o * lax.rsqrt(ms + EPS) * sg) * (1.0 - lam_init)


def _attend(q_ref, key_refs, value_refs, lam, sg, lam_init, o_ref):
    def head(ref, h):
        return ref(slice(h * LANES, (h + 1) * LANES))

    nxt = _head_scores(head(q_ref, 0), [head(k, 0) for k in key_refs])
    for h in range(A_HEADS):
        cur = nxt
        if h + 1 < A_HEADS:
            nxt = _head_scores(head(q_ref, h + 1), [head(k, h + 1) for k in key_refs])
        o = _head_output(cur, [head(v, h) for v in value_refs], lam, sg, lam_init)
        o_ref[:, h * LANES:(h + 1) * LANES] = o.astype(BF16)


def _attn_ctx_kernel(q_ref, k_ref, v_ref, lq1, lk1, lq2, lk2, sg_ref, o_ref, *, lam_init):
    lam = _lambda(lq1, lk1, lq2, lk2, lam_init)
    _attend(lambda sl: q_ref[:, sl], [lambda sl: k_ref[:, sl]], [lambda sl: v_ref[:, sl]],
            lam, sg_ref[...], lam_init, o_ref)


def _attn_lat_kernel(q_ref, k_ref, v_ref, ck_ref, cv_ref, lq1, lk1, lq2, lk2, sg_ref, oin_ref, o_ref,
                     *, lam_init):
    del oin_ref
    lam = _lambda(lq1, lk1, lq2, lk2, lam_init)
    _attend(lambda sl: q_ref[:, sl],
            [lambda sl: k_ref[:, sl], lambda sl: ck_ref[0, 0, :, sl]],
            [lambda sl: v_ref[:, sl], lambda sl: cv_ref[0, 0, :, sl]],
            lam, sg_ref[...], lam_init, o_ref)


def _attention_call(l, proj, cache_k4, cache_v4, lams, sg, lam_init, dims):
    NC, NL, S, DS = dims
    M = NC + NL
    B = NC // S
    DB = NL // DS
    lam_specs1 = [pl.BlockSpec((1, A_DH), lambda b: (0, 0))] * 4
    oa = pl.pallas_call(
        functools.partial(_attn_ctx_kernel, lam_init=lam_init),
        grid=(B,),
        in_specs=[
            pl.BlockSpec((S, A_WIDTH), lambda b: (b, J_Q)),
            pl.BlockSpec((S, A_WIDTH), lambda b: (b, J_K)),
            pl.BlockSpec((S, A_WIDTH), lambda b: (b, J_V)),
            *lam_specs1,
            pl.BlockSpec((1, LANES), lambda b: (0, 0)),
        ],
        out_specs=pl.BlockSpec((S, A_WIDTH), lambda b: (b, 0)),
        out_shape=jax.ShapeDtypeStruct((M, A_WIDTH), BF16),
        compiler_params=_cparams(("arbitrary",), 32),
        name="attn_ctx",
    )(proj, proj, proj, *lams, sg)

    nq = DS // TQ
    q0 = NC // TQ
    k0 = NC // DS
    P = cache_k4.shape[2]
    lam_specs2 = [pl.BlockSpec((1, A_DH), lambda b, qi: (0, 0))] * 4
    oa = pl.pallas_call(
        functools.partial(_attn_lat_kernel, lam_init=lam_init),
        grid=(DB, nq),
        in_specs=[
            pl.BlockSpec((TQ, A_WIDTH), lambda b, qi: (q0 + b * nq + qi, J_Q)),
            pl.BlockSpec((DS, A_WIDTH), lambda b, qi: (k0 + b, J_K)),
            pl.BlockSpec((DS, A_WIDTH), lambda b, qi: (k0 + b, J_V)),
            pl.BlockSpec((1, 1, P, A_WIDTH), lambda b, qi: (b, l, 0, 0)),
            pl.BlockSpec((1, 1, P, A_WIDTH), lambda b, qi: (b, l, 0, 0)),
            *lam_specs2,
            pl.BlockSpec((1, LANES), lambda b, qi: (0, 0)),
            pl.BlockSpec(memory_space=pl.ANY),
        ],
        out_specs=pl.BlockSpec((TQ, A_WIDTH), lambda b, qi: (q0 + b * nq + qi, 0)),
        out_shape=jax.ShapeDtypeStruct((M, A_WIDTH), BF16),
        input_output_aliases={10: 0},
        compiler_params=_cparams(("arbitrary", "arbitrary"), 48),
        name="attn_lat",
    )(proj, proj, proj, cache_k4, cache_v4, *lams, sg, oa)
    return oa


def _mix_kernel(pc_ref, bc_ref, wp_ref, ps_ref, cw_ref, cb_ref, o_ref, *, n_ctx_tiles, S, DS):
    i = pl.program_id(0)
    tm = pc_ref.shape[0]
    nb = tm // MIX_BLK
    is_lat = i >= n_ctx_tiles
    seqlen = jnp.where(is_lat, DS, S)
    rows = lax.broadcasted_iota(I32, (tm, 1), 0)
    pos = rows & (seqlen - 1)

    t = lax.broadcasted_iota(I32, (MIX_BLK, MIX_BLK), 0)
    s = lax.broadcasted_iota(I32, (MIX_BLK, MIX_BLK), 1)
    for g, win in enumerate(P_WINDOWS):
        half = win // 2
        sl = slice(g * P_GC, (g + 1) * P_GC)
        d_cur = s - t
        band_cur = jnp.logical_and(d_cur >= -half, d_cur < half).astype(BF16)
        d_prev = d_cur - MIX_BLK
        band_prev = jnp.logical_and(is_lat, jnp.logical_and(d_prev >= -half, d_prev < half)).astype(BF16)
        d_next = d_cur + MIX_BLK
        band_next = jnp.logical_and(is_lat, jnp.logical_and(d_next >= -half, d_next < half)).astype(BF16)
        for b in range(nb):
            r0 = b * MIX_BLK
            u_b = pc_ref[r0:r0 + MIX_BLK, sl]
            acc = _dot(band_cur, u_b)
            if b > 0:
                acc = acc + _dot(band_prev, pc_ref[r0 - MIX_BLK:r0, sl])
            if b + 1 < nb:
                acc = acc + _dot(band_next, pc_ref[r0 + MIX_BLK:r0 + 2 * MIX_BLK, sl])
            p = pos[r0:r0 + MIX_BLK]
            inv_cnt = 1.0 / (jnp.minimum(p + half, seqlen) - jnp.maximum(p - half, 0)).astype(F32)
            y = (acc * inv_cnt - u_b.astype(F32)).astype(BF16)
            o_ref[r0:r0 + MIX_BLK, sl] = (_dot(y, wp_ref[g]) * ps_ref[:, sl]).astype(BF16)

    u = pc_ref[:, P_WIDTH:].astype(F32)
    gate_b = bc_ref[:, :C_WIDTH].astype(F32)
    gate_c = bc_ref[:, C_WIDTH:].astype(F32)
    z = gate_c * u
    zm = jnp.where(pos >= 1, pltpu.roll(z, 1, 0), 0.0)
    zp = jnp.where(pos + 1 < seqlen, pltpu.roll(z, tm - 1, 0), 0.0)
    conv = zm * cw_ref[0:1, :] + z * cw_ref[1:2, :] + zp * cw_ref[2:3, :] + cb_ref[...]
    o_ref[:, P_WIDTH:] = (gate_b * conv).astype(BF16)


def _mix_call(proj, w_pool_l, pool_scale_l, conv_w_l, conv_b_l, dims):
    NC, NL, S, DS = dims
    M = NC + NL
    tm = TM_MIX
    assert S == MIX_BLK and tm == DS and NC % tm == 0 and max(P_WINDOWS) // 2 <= MIX_BLK
    kern = functools.partial(_mix_kernel, n_ctx_tiles=NC // tm, S=S, DS=DS)
    return pl.pallas_call(
        kern,
        grid=(M // tm,),
        in_specs=[
            pl.BlockSpec((tm, COL_BLK), lambda i: (i, J_PC)),
            pl.BlockSpec((tm, COL_BLK), lambda i: (i, J_BC)),
            pl.BlockSpec((P_GROUPS, P_GC, P_GC), lambda i: (0, 0, 0)),
            pl.BlockSpec((1, P_WIDTH), lambda i: (0, 0)),
            pl.BlockSpec((3, C_WIDTH), lambda i: (0, 0)),
            pl.BlockSpec((1, C_WIDTH), lambda i: (0, 0)),
        ],
        out_specs=pl.BlockSpec((tm, P_WIDTH + C_WIDTH), lambda i: (i, 0)),
        out_shape=jax.ShapeDtypeStruct((M, P_WIDTH + C_WIDTH), BF16),
        compiler_params=_cparams(("arbitrary",), 48),
        name="mixers",
    )(proj, proj, w_pool_l, pool_scale_l, conv_w_l, conv_b_l)


def _route(logits):
    lane = lax.broadcasted_iota(I32, logits.shape, 1)
    lane_f = lane.astype(F32)
    neg = -jnp.inf

    def first_argmax(v, vmax):
        return jnp.min(jnp.where(v == vmax, lane_f, float(LANES)), axis=-1, keepdims=True).astype(I32)

    gl = jnp.where(lane < N_GROUPS, logits, neg)
    gmax = jnp.max(gl, axis=-1, keepdims=True)
    gsel = first_argmax(gl, gmax)
    g_w = 1.0 / jnp.sum(jnp.exp(gl - gmax), axis=-1, keepdims=True)

    first = N_GROUPS + gsel * E_PER_GROUP
    in_grp = jnp.logical_and(lane >= first, lane < first + E_PER_GROUP)
    el = jnp.where(in_grp, logits, neg)
    t1 = jnp.max(el, axis=-1, keepdims=True)
    i1 = first_argmax(el, t1)
    el2 = jnp.where(lane == i1, neg, el)
    t2 = jnp.max(el2, axis=-1, keepdims=True)
    i2 = first_argmax(el2, t2)
    a = jnp.exp(t2 - t1)
    w1 = g_w / (1.0 + a)
    w2 = g_w * a / (1.0 + a)

    swap = i2 < i1
    llo = jnp.where(swap, i2, i1) - first
    lhi = jnp.where(swap, i1, i2) - first
    w_lo = jnp.where(swap, w2, w1)
    w_hi = jnp.where(swap, w1, w2)
    pair = ((llo * (7 - llo)) >> 1) + (lhi - llo - 1)
    cls = gsel * N_PAIRS + pair
    wvec = jnp.where(lane == 0, w_lo, jnp.where(lane == 1, w_hi, 0.0))
    return cls, wvec


def _post_kernel(x_ref, oa_ref, opc_ref, g0_ref, g1_ref, g2_ref, mod_ref, n2_ref, wa_ref, wp_ref, wc_ref,
                 wo_ref, wr_ref, br_ref, x1_ref, h2w_ref, cls_ref):
    a = _dot(oa_ref[...], wa_ref[0])
    merged = g0_ref[...].astype(F32) * a
    p = _dot(opc_ref[:, :P_WIDTH], wp_ref[0])
    merged = merged + g1_ref[...].astype(F32) * p
    c = _dot(opc_ref[:, P_WIDTH:], wc_ref[0])
    merged = merged + g2_ref[...].astype(F32) * c
    y = _dot(merged.astype(BF16), wo_ref[0])
    x1 = x_ref[...] + mod_ref[0, 2:3, :] * y
    x1_ref[...] = x1
    ms = jnp.mean(x1 * x1, axis=-1, keepdims=True)
    h2 = x1 * lax.rsqrt(ms + EPS) * n2_ref[...]
    h2 = h2 * (1.0 + mod_ref[0, 4:5, :]) + mod_ref[0, 3:4, :]
    h2b = h2.astype(BF16)
    h2w_ref[:, :HALF] = _pack_halves(h2b)
    logits = _dot(h2b, wr_ref[0]) + br_ref[0]
    cls, wvec = _route(logits)
    h2w_ref[:, HALF:] = lax.bitcast_convert_type(wvec, U32)
    cls_ref[...] = jnp.broadcast_to(cls, cls_ref.shape)


def _post_call(l, x_all, oa, opc, proj, mod_l, g2, wa, wp, wc, wo, wr, br, dims):
    NC, NL, S, DS = dims
    M = NC + NL
    tm = TM_POST
    nct = NC // tm
    tpl = DS // tm

    def modrow(i):
        return jnp.where(i < nct, 0, 1 + (i - nct) // tpl)

    const3 = lambda i: (l, 0, 0)
    single = pl.Buffered(1)
    return pl.pallas_call(
        _post_kernel,
        grid=(M // tm,),
        in_specs=[
            pl.BlockSpec((tm, D), lambda i: (i, 0)),
            pl.BlockSpec((tm, A_WIDTH), lambda i: (i, 0)),
            pl.BlockSpec((tm, P_WIDTH + C_WIDTH), lambda i: (i, 0)),
            pl.BlockSpec((tm, D), lambda i: (i, 0)),
            pl.BlockSpec((tm, D), lambda i: (i, 1)),
            pl.BlockSpec((tm, D), lambda i: (i, 2)),
            pl.BlockSpec((1, N_MOD, D), lambda i: (modrow(i), 0, 0)),
            pl.BlockSpec((1, D), lambda i: (0, 0)),
            pl.BlockSpec((1, A_WIDTH, D), const3, pipeline_mode=single),
            pl.BlockSpec((1, P_WIDTH, D), const3, pipeline_mode=single),
            pl.BlockSpec((1, C_WIDTH, D), const3, pipeline_mode=single),
            pl.BlockSpec((1, D, D), const3, pipeline_mode=single),
            pl.BlockSpec((1, D, LANES), const3, pipeline_mode=single),
            pl.BlockSpec((1, 1, LANES), const3),
        ],
        out_specs=[
            pl.BlockSpec((tm, D), lambda i: (i, 0)),
            pl.BlockSpec((tm, XS_COLS), lambda i: (i, 0)),
            pl.BlockSpec((tm, LANES), lambda i: (i, 0)),
        ],
        out_shape=[
            jax.ShapeDtypeStruct((M, D), F32),
            jax.ShapeDtypeStruct((M, XS_COLS), U32),
            jax.ShapeDtypeStruct((M, LANES), I32),
        ],
        compiler_params=_cparams(("arbitrary",), 52),
        name="post_attn",
    )(x_all, oa, opc, proj, proj, proj, mod_l, g2, wa, wp, wc, wo, wr, br)


def _sort_plan(cls, n_chunks):
    tm = TM_MOE
    onehot = (cls[:, None] == jnp.arange(N_CLASSES, dtype=I32)[None, :]).astype(I32)
    csum = jnp.cumsum(onehot, axis=0)
    rank = jnp.sum(onehot * csum, axis=1) - 1
    counts = csum[-1]
    padded = ((counts + tm - 1) // tm) * tm
    ends = jnp.cumsum(padded)
    starts = ends - padded
    pos = (jnp.sum(onehot * starts[None, :], axis=1) + rank).astype(I32)
    used = ends[-1] // tm
    chunk = jnp.arange(n_chunks, dtype=I32)
    cidx = jnp.minimum(chunk, used - 1)
    ccls = jnp.sum((ends[None, :] <= (cidx * tm)[:, None]).astype(I32), axis=1)
    ea = jnp.asarray(_CLASS_EA)[ccls]
    eb = jnp.asarray(_CLASS_EB)[ccls]
    valid = (chunk < used).astype(I32)
    return pos, cidx.astype(I32), ea, eb, valid


def _dispatch_kernel(pos_ref, h2w_ref, xs_in_ref, xs_ref, sem):
    del xs_in_ref
    tm = h2w_ref.shape[0]
    base = pl.program_id(0) * tm

    for r in range(tm):
        dst = pos_ref[base + r]
        pltpu.make_async_copy(h2w_ref.at[pl.ds(r, 1)], xs_ref.at[pl.ds(dst, 1)],
                              sem).start(priority=r % N_DMA_PRIORITIES)

    def drain(r, carry):
        pltpu.make_async_copy(h2w_ref.at[pl.ds(0, 1)], xs_ref.at[pl.ds(0, 1)], sem).wait()
        return carry

    lax.fori_loop(0, tm, drain, 0, unroll=8)


def _dispatch_call(pos, h2w, xs):
    M = h2w.shape[0]
    tm = TM_DISP
    return pl.pallas_call(
        _dispatch_kernel,
        grid_spec=pltpu.PrefetchScalarGridSpec(
            num_scalar_prefetch=1,
            grid=(M // tm,),
            in_specs=[
                pl.BlockSpec((tm, XS_COLS), lambda i, pos: (i, 0)),
                pl.BlockSpec(memory_space=pl.ANY),
            ],
            out_specs=pl.BlockSpec(memory_space=pl.ANY),
            scratch_shapes=[pltpu.SemaphoreType.DMA(())],
        ),
        out_shape=jax.ShapeDtypeStruct(xs.shape, U32),
        input_output_aliases={2: 0},
        compiler_params=_cparams(("arbitrary",), 32),
        name="dispatch",
    )(pos, h2w, xs)


def _moe_kernel(cidx_ref, ea_ref, eb_ref, valid_ref, xs_ref, wga, wua, wda, wgb, wub, wdb, ys_ref):
    del cidx_ref, ea_ref, eb_ref
    c = pl.program_id(0)

    @pl.when(valid_ref[c] == 1)
    def _():
        xa, xb = _unpack_halves(xs_ref[:, :HALF])
        x = jnp.concatenate([xa.astype(BF16), xb.astype(BF16)], axis=1)
        wts = lax.bitcast_convert_type(xs_ref[:, HALF:], F32)

        def expert(wg, wu, wd, w):
            g = _dot(x, wg[0, 0])
            u = _dot(x, wu[0, 0])
            act = (g * jax.nn.sigmoid(g)) * u
            return _dot((act * w).astype(BF16), wd[0, 0])

        y = expert(wga, wua, wda, wts[:, 0:1]) + expert(wgb, wub, wdb, wts[:, 1:2])
        ys_ref[...] = _pack_halves(y.astype(BF16))


def _moe_call(l, cidx, ea, eb, valid, xs, wg, wu, wd):
    n_chunks = cidx.shape[0]
    tm = TM_MOE
    up_a = pl.BlockSpec((1, 1, D, D_EXPERT), lambda c, ci, a, b, v: (l, a[c], 0, 0))
    up_b = pl.BlockSpec((1, 1, D, D_EXPERT), lambda c, ci, a, b, v: (l, b[c], 0, 0))
    dn_a = pl.BlockSpec((1, 1, D_EXPERT, D), lambda c, ci, a, b, v: (l, a[c], 0, 0))
    dn_b = pl.BlockSpec((1, 1, D_EXPERT, D), lambda c, ci, a, b, v: (l, b[c], 0, 0))
    return pl.pallas_call(
        _moe_kernel,
        grid_spec=pltpu.PrefetchScalarGridSpec(
            num_scalar_prefetch=4,
            grid=(n_chunks,),
            in_specs=[
                pl.BlockSpec((tm, XS_COLS), lambda c, ci, a, b, v: (ci[c], 0)),
                up_a, up_a, dn_a, up_b, up_b, dn_b,
            ],
            out_specs=pl.BlockSpec((tm, HALF), lambda c, ci, a, b, v: (ci[c], 0)),
        ),
        out_shape=jax.ShapeDtypeStruct((n_chunks * tm, HALF), U32),
        compiler_params=_cparams(("arbitrary",), 48),
        name="moe_experts",
    )(cidx, ea, eb, valid, xs, wg, wu, wd, wg, wu, wd)


def _combine_kernel(pos_ref, x1_ref, mod_ref, ys_ref, *rest, n_ctx_tiles, split):
    if split:
        op_ref, os_ref, ybuf, sem = rest
    else:
        o_ref, ybuf, sem = rest
    tm = x1_ref.shape[0]
    i = pl.program_id(0)
    n = pl.num_programs(0)

    def issue(step, slot):
        base = step * tm
        for r in range(tm):
            src = pos_ref[base + r]
            pltpu.make_async_copy(ys_ref.at[pl.ds(src, 1)], ybuf.at[slot, pl.ds(r, 1)],
                                  sem.at[slot]).start(priority=r % N_DMA_PRIORITIES)

    @pl.when(i == 0)
    def _():
        issue(0, 0)

    for nxt_slot in range(2):
        @pl.when(jnp.logical_and(i + 1 < n, (i + 1) % 2 == nxt_slot))
        def _():
            issue(i + 1, nxt_slot)

    slot = i % 2

    def drain(r, carry):
        pltpu.make_async_copy(ys_ref.at[pl.ds(0, 1)], ybuf.at[slot, pl.ds(0, 1)], sem.at[slot]).wait()
        return carry

    lax.fori_loop(0, tm, drain, 0, unroll=8)
    ya, yb = _unpack_halves(ybuf[slot])
    out = x1_ref[...] + mod_ref[0, 5:6, :] * jnp.concatenate([ya, yb], axis=1)
    if split:
        @pl.when(i < n_ctx_tiles)
        def _():
            op_ref[...] = out

        @pl.when(i >= n_ctx_tiles)
        def _():
            os_ref[...] = out
    else:
        o_ref[...] = out


def _combine_call(pos, x1, mod_l, ys, dims, split):
    NC, NL, S, DS = dims
    M = NC + NL
    tm = TM_COMB
    nct = NC // tm
    tpl = DS // tm

    def modrow(i, pos):
        return (jnp.where(i < nct, 0, 1 + (i - nct) // tpl), 0, 0)

    if split:
        out_specs = [pl.BlockSpec((tm, D), lambda i, pos: (jnp.minimum(i, nct - 1), 0)),
                     pl.BlockSpec((tm, D), lambda i, pos: (jnp.maximum(i - nct, 0), 0))]
        out_shape = [jax.ShapeDtypeStruct((NC, D), F32), jax.ShapeDtypeStruct((NL, D), F32)]
    else:
        out_specs = pl.BlockSpec((tm, D), lambda i, pos: (i, 0))
        out_shape = jax.ShapeDtypeStruct((M, D), F32)
    return pl.pallas_call(
        functools.partial(_combine_kernel, n_ctx_tiles=nct, split=split),
        grid_spec=pltpu.PrefetchScalarGridSpec(
            num_scalar_prefetch=1,
            grid=(M // tm,),
            in_specs=[
                pl.BlockSpec((tm, D), lambda i, pos: (i, 0)),
                pl.BlockSpec((1, N_MOD, D), modrow),
                pl.BlockSpec(memory_space=pl.ANY),
            ],
            out_specs=out_specs,
            scratch_shapes=[pltpu.VMEM((2, tm, HALF), U32), pltpu.SemaphoreType.DMA((2,))],
        ),
        out_shape=out_shape,
        compiler_params=_cparams(("arbitrary",), 32),
        name="combine",
    )(pos, x1, mod_l, ys)


def _rope_tables(DS, tm):
    rows = DS // GRID_W
    pos_r = jnp.repeat(jnp.arange(rows, dtype=F32), GRID_W)
    pos_c = jnp.tile(jnp.arange(GRID_W, dtype=F32), rows)
    inv_freq = jnp.power(ROPE_THETA, -jnp.arange(ROPE_FREQ, dtype=F32) / ROPE_FREQ)
    ang = jnp.stack([pos_r[:, None] * inv_freq, pos_c[:, None] * inv_freq], axis=1)
    cos, sin = jnp.cos(ang), jnp.sin(ang)
    zeros = jnp.zeros_like(sin)
    c = jnp.tile(jnp.stack([cos, cos], axis=2).reshape(DS, A_DH), (1, 2))
    sm = jnp.tile(jnp.stack([-sin, zeros], axis=2).reshape(DS, A_DH), (1, 2))
    sp = jnp.tile(jnp.stack([zeros, sin], axis=2).reshape(DS, A_DH), (1, 2))
    ident = jnp.ones((tm, LANES), F32)
    zpad = jnp.zeros((tm, LANES), F32)
    return (jnp.concatenate([ident, c], 0), jnp.concatenate([zpad, sm], 0), jnp.concatenate([zpad, sp], 0))


def _qk_tables(rope, q_gain, k_gain):
    c, sm, sp = rope
    partner = np.arange(LANES) ^ ROPE_FREQ
    out = []
    for gain, scale in ((q_gain, A_DH ** -0.5 * math.log2(math.e)), (k_gain, 1.0)):
        g = jnp.tile(gain, 2) * scale
        out += [c * g, (sm + sp) * g[partner]]
    return jnp.stack(out, axis=0)


def kernel(x_prompt, x_sample, cache_k, cache_v, c, c_ctx, w_ada, b_ada, norm1_g, norm2_g, w_in, q_norm_g,
           k_norm_g, lam_q1, lam_k1, lam_q2, lam_k2, subln_g, w_pool, pool_scale, conv_w, conv_b, w_br_a,
           w_br_p, w_br_c, w_out, w_route_group, b_route_group, w_route_expert, b_route_expert, w_exp_gate,
           w_exp_up, w_exp_down):
    B, S, _ = x_prompt.shape
    DB, DS, _ = x_sample.shape
    L = w_in.shape[0]
    P = cache_k.shape[2]
    NC, NL = B * S, DB * DS
    M = NC + NL
    dims = (NC, NL, S, DS)
    assert DB + 1 <= 8 and NC % DS == 0 and NC % TM_IN == 0 and DS % TM_IN == 0 and TM_IN % S == 0
    assert S & (S - 1) == 0 and DS & (DS - 1) == 0

    cond8 = jnp.concatenate([c_ctx[None], c, jnp.zeros((8 - 1 - DB, D), F32)], axis=0)
    mod = _ada_call(cond8, w_ada, b_ada[:, None, :]).reshape(L, 8, N_MOD, D)

    w_in_b = w_in.astype(BF16)
    wa, wp, wc, wo = (w.astype(BF16) for w in (w_br_a, w_br_p, w_br_c, w_out))
    wg, wu, wd = (w.astype(BF16) for w in (w_exp_gate, w_exp_up, w_exp_down))
    wpool = w_pool.astype(BF16)
    wr = jnp.concatenate([w_route_group, w_route_expert,
                          jnp.zeros((L, D, LANES - N_GROUPS - N_EXPERTS), F32)], axis=2).astype(BF16)
    br = jnp.concatenate([b_route_group, b_route_expert,
                          jnp.zeros((L, LANES - N_GROUPS - N_EXPERTS), F32)], axis=1)[:, None, :]

    rope = _rope_tables(DS, TM_IN)
    cache_k4 = cache_k.astype(BF16).reshape(DB, L, P, A_WIDTH)
    cache_v4 = cache_v.astype(BF16).reshape(DB, L, P, A_WIDTH)

    x_all = jnp.concatenate([x_prompt.reshape(NC, D), x_sample.reshape(NL, D)], axis=0)
    kc = vc = None
    n_chunks = (M + N_CLASSES * (TM_MOE - 1)) // TM_MOE + 1
    xs = jnp.zeros((n_chunks * TM_MOE, XS_COLS), U32)

    for l in range(L):
        lam_init = 0.8 - 0.6 * math.exp(-0.3 * l)
        mod_l = mod[l]
        tabs = _qk_tables(rope, q_norm_g[l], k_norm_g[l])
        proj, kc, vc = _inproj_call(l, x_all, mod_l, norm1_g[l][None], w_in_b, tabs, kc, vc,
                                    (B, L, S, A_WIDTH), dims)
        lams = [a[l][None] for a in (lam_q1, lam_k1, lam_q2, lam_k2)]
        oa = _attention_call(l, proj, cache_k4, cache_v4, lams, subln_g[l][None], lam_init, dims)
        opc = _mix_call(proj, wpool[l], pool_scale[l][None], conv_w[l], conv_b[l][None], dims)
        x1, h2w, cls = _post_call(l, x_all, oa, opc, proj, mod_l, norm2_g[l][None],
                                  wa, wp, wc, wo, wr, br, dims)
        pos, cidx, ea, eb, valid = _sort_plan(cls[:, 0], n_chunks)
        xs = _dispatch_call(pos, h2w, xs)
        ys = _moe_call(l, cidx, ea, eb, valid, xs, wg, wu, wd)
        x_all = _combine_call(pos, x1, mod_l, ys, dims, split=(l == L - 1))

    y_p, y_s = x_all
    return (y_p.reshape(B, S, D), y_s.reshape(DB, DS, D),
            kc.reshape(B, L, S, A_HEADS, 2, A_DH), vc.reshape(B, L, S, A_HEADS, A_DV))
```

```python
import functools
import math

import numpy as np
import jax
import jax.numpy as jnp
from jax import lax
from jax.experimental import pallas as pl
from jax.experimental.pallas import tpu as pltpu

F32 = jnp.float32
BF16 = jnp.bfloat16
I32 = jnp.int32
U32 = jnp.uint32

D = 2048
N_MOD = 6
EPS = 1e-6
GRID_W = 64
A_WIDTH = 1024
A_HEADS = 8
A_DH = 64
A_DV = 128
ROPE_FREQ = 16
ROPE_THETA = 10000.0
P_WIDTH = 512
P_GROUPS = 4
P_GC = 128
P_WINDOWS = (2, 4, 8, 16)
C_WIDTH = 512
N_GROUPS = 4
E_PER_GROUP = 4
N_EXPERTS = 16
D_EXPERT = 512

LANES = 128
MXU_DIM = 256
N_DMA_PRIORITIES = 2
PROJ_COLS = 11264
COL_BLK = 1024
N_GATE_BLKS = 6
J_Q, J_K, J_V, J_PC, J_BC = 6, 7, 8, 9, 10
W_IN_BLKS = PROJ_COLS // COL_BLK
N_PAIRS = 6
N_CLASSES = N_GROUPS * N_PAIRS
HALF = D // 2
XS_COLS = HALF + LANES

TM_IN = 1024
TQ = 256
TM_MIX = 2048
MIX_BLK = 256
TM_POST = 256
TM_DISP = 1024
TM_MOE = 256
TM_COMB = 512

_PAIRS = [(a, b) for a in range(E_PER_GROUP) for b in range(a + 1, E_PER_GROUP)]
_CLASS_EA = np.array([g * E_PER_GROUP + _PAIRS[p][0] for g in range(N_GROUPS) for p in range(N_PAIRS)], np.int32)
_CLASS_EB = np.array([g * E_PER_GROUP + _PAIRS[p][1] for g in range(N_GROUPS) for p in range(N_PAIRS)], np.int32)


def _cparams(sem, vmem_mb):
    return pltpu.CompilerParams(dimension_semantics=sem, vmem_limit_bytes=vmem_mb * 1024 * 1024)


def _dot(a, b):
    return jnp.dot(a, b, preferred_element_type=F32)


def _dot_nt(a, b):
    return lax.dot_general(a, b, (((1,), (1,)), ((), ())), preferred_element_type=F32)


def _sigmoid(x):
    return 0.5 * jnp.tanh(0.5 * x) + 0.5


def _pack_halves(xb):
    hi = lax.bitcast_convert_type(xb[:, :HALF].astype(F32), U32)
    lo = lax.bitcast_convert_type(xb[:, HALF:].astype(F32), U32)
    return hi | (lo >> 16)


def _unpack_halves(words):
    hi = lax.bitcast_convert_type(words & jnp.uint32(0xFFFF0000), F32)
    lo = lax.bitcast_convert_type(words << 16, F32)
    return hi, lo


def _ada_kernel(c_ref, w_ref, b_ref, o_ref):
    c = c_ref[...]
    a = (c * jax.nn.sigmoid(c)).astype(BF16)
    o_ref[0] = _dot(a, w_ref[0].astype(BF16)) + b_ref[0]


def _ada_call(cond8, w_ada, b_ada3):
    L = w_ada.shape[0]
    tn = 1024
    return pl.pallas_call(
        _ada_kernel,
        grid=(L, N_MOD * D // tn),
        in_specs=[
            pl.BlockSpec((8, D), lambda l, j: (0, 0)),
            pl.BlockSpec((1, D, tn), lambda l, j: (l, 0, j)),
            pl.BlockSpec((1, 1, tn), lambda l, j: (l, 0, j)),
        ],
        out_specs=pl.BlockSpec((1, 8, tn), lambda l, j: (l, 0, j)),
        out_shape=jax.ShapeDtypeStruct((L, 8, N_MOD * D), F32),
        compiler_params=_cparams(("arbitrary", "arbitrary"), 40),
        name="ada_mod",
    )(cond8, w_ada, b_ada3)


def _split_bf16(x):
    hi = x.astype(BF16)
    lo = (x - hi.astype(F32)).astype(BF16)
    return hi, lo


def _group_rsqrt(acc, ones_ref):
    hi, lo = _split_bf16(acc * acc)
    blk = ones_ref.shape[0]
    parts = []
    for b in range(acc.shape[1] // blk):
        sl = slice(b * blk, (b + 1) * blk)
        parts.append(_dot(hi[:, sl], ones_ref[...]) + _dot(lo[:, sl], ones_ref[...]))
    return [lax.rsqrt(p * (1.0 / A_DH) + EPS) for p in parts]


def _swap16(x):
    lane = lax.broadcasted_iota(I32, x.shape, 1)
    up = pltpu.roll(x, LANES - ROPE_FREQ, 1)
    dn = pltpu.roll(x, ROPE_FREQ, 1)
    return jnp.where((lane & ROPE_FREQ) == 0, up, dn)


def _inproj_kernel(*refs, n_ctx_tiles, aliased, layer):
    if aliased:
        (x_ref, mod_ref, g1_ref, w_ref, tab_ref, ones_ref, _, _,
         proj_ref, kc_ref, vc_ref, h_ref, kv_buf, kv_sem) = refs
    else:
        (x_ref, mod_ref, g1_ref, w_ref, tab_ref, ones_ref,
         proj_ref, kc_ref, vc_ref, h_ref, kv_buf, kv_sem) = refs
    i = pl.program_id(0)
    j = pl.program_id(1)
    spt = kv_buf.shape[0]
    is_ctx = i < n_ctx_tiles

    def matmul():
        return _dot(h_ref[...], w_ref[0])

    def cache_copy(dst_ref):
        return pltpu.make_async_copy(kv_buf, dst_ref.at[pl.ds(i * spt, spt), layer], kv_sem)

    def qk_epilogue(acc, t0):
        rs = _group_rsqrt(acc, ones_ref)
        heads_per_blk = ones_ref.shape[0] // LANES
        out = []
        for h in range(A_HEADS):
            sl = slice(h * LANES, (h + 1) * LANES)
            x = acc[:, sl]
            r = rs[h // heads_per_blk][:, (h % heads_per_blk) * LANES:(h % heads_per_blk + 1) * LANES]
            out.append((x * tab_ref[t0] + _swap16(x) * tab_ref[t0 + 1]) * r)
        return out

    @pl.when(j == 0)
    def _():
        x = x_ref[...]
        ms = jnp.mean(x * x, axis=-1, keepdims=True)
        y = x * lax.rsqrt(ms + EPS) * g1_ref[...]
        h = y * (1.0 + mod_ref[0, 1:2, :]) + mod_ref[0, 0:1, :]
        h_ref[...] = h.astype(BF16)
        proj_ref[...] = _sigmoid(matmul()).astype(BF16)

    @pl.when(jnp.logical_and(j > 0, j < N_GATE_BLKS))
    def _():
        proj_ref[...] = _sigmoid(matmul()).astype(BF16)

    @pl.when(j == J_Q)
    def _():
        ys = qk_epilogue(matmul(), 0)
        for h in range(A_HEADS):
            proj_ref[:, h * LANES:(h + 1) * LANES] = ys[h].astype(BF16)

    @pl.when(j == J_K)
    def _():
        ys = qk_epilogue(matmul(), 2)
        for h in range(A_HEADS):
            proj_ref[:, h * LANES:(h + 1) * LANES] = ys[h].astype(BF16)

        @pl.when(is_ctx)
        def _():
            for h in range(A_HEADS):
                kv_buf[:, :, h * LANES:(h + 1) * LANES] = ys[h].reshape(spt, kv_buf.shape[1], LANES)
            cache_copy(kc_ref).start()

    @pl.when(j == J_V)
    def _():
        acc = matmul()
        proj_ref[...] = acc.astype(BF16)

        @pl.when(is_ctx)
        def _():
            cache_copy(kc_ref).wait()
            kv_buf[...] = acc.reshape(kv_buf.shape)
            cache_copy(vc_ref).start()

    @pl.when(j > J_V)
    def _():
        proj_ref[...] = matmul().astype(BF16)

        @pl.when(jnp.logical_and(is_ctx, j == J_V + 1))
        def _():
            cache_copy(vc_ref).wait()


def _group_ones():
    idx = np.arange(MXU_DIM) // A_DH
    return jnp.asarray((idx[:, None] == idx[None, :]).astype(np.float32), BF16)


def _inproj_call(l, x_all, mod_l, g1, w_in_b, tabs, kc, vc, cache_shape, dims):
    NC, NL, S, DS = dims
    M = NC + NL
    tm = TM_IN
    nct = NC // tm
    tpl = DS // tm
    seq_per_tile = tm // S
    aliased = kc is not None
    ones = _group_ones()

    def modrow(i):
        return jnp.where(i < nct, 0, 1 + (i - nct) // tpl)

    def roperow(i):
        return jnp.where(i < nct, 0, 1 + (i - nct) % tpl)

    def wcol(j):
        return jnp.where(j < N_GATE_BLKS, j + (W_IN_BLKS - N_GATE_BLKS), j - N_GATE_BLKS)

    in_specs = [
        pl.BlockSpec((tm, D), lambda i, j: (i, 0)),
        pl.BlockSpec((1, N_MOD, D), lambda i, j: (modrow(i), 0, 0)),
        pl.BlockSpec((1, D), lambda i, j: (0, 0)),
        pl.BlockSpec((1, D, COL_BLK), lambda i, j: (l, 0, wcol(j))),
        pl.BlockSpec((4, tm, LANES), lambda i, j: (0, roperow(i), 0)),
        pl.BlockSpec((MXU_DIM, MXU_DIM), lambda i, j: (0, 0)),
    ]
    args = [x_all, mod_l, g1, w_in_b, tabs, ones]
    aliases = {}
    if aliased:
        in_specs += [pl.BlockSpec(memory_space=pl.ANY), pl.BlockSpec(memory_space=pl.ANY)]
        args += [kc, vc]
        aliases = {6: 1, 7: 2}
    kern = functools.partial(_inproj_kernel, n_ctx_tiles=nct, aliased=aliased, layer=l)
    return pl.pallas_call(
        kern,
        grid=(M // tm, PROJ_COLS // COL_BLK),
        in_specs=in_specs,
        out_specs=[
            pl.BlockSpec((tm, COL_BLK), lambda i, j: (i, j)),
            pl.BlockSpec(memory_space=pl.ANY),
            pl.BlockSpec(memory_space=pl.ANY),
        ],
        out_shape=[
            jax.ShapeDtypeStruct((M, PROJ_COLS), BF16),
            jax.ShapeDtypeStruct(cache_shape, F32),
            jax.ShapeDtypeStruct(cache_shape, F32),
        ],
        scratch_shapes=[pltpu.VMEM((tm, D), BF16), pltpu.VMEM((seq_per_tile, S, A_WIDTH), F32),
                        pltpu.SemaphoreType.DMA(())],
        input_output_aliases=aliases,
        compiler_params=_cparams(("arbitrary", "arbitrary"), 56),
        name="in_proj",
    )(*args)


def _lambda(lq1, lk1, lq2, lk2, lam_init):
    a = jnp.sum(lq1[...] * lk1[...], axis=-1, keepdims=True)
    b = jnp.sum(lq2[...] * lk2[...], axis=-1, keepdims=True)
    return jnp.exp(a) - jnp.exp(b) + lam_init


def _head_scores(q_h, ks):
    lane = lax.broadcasted_iota(I32, q_h.shape, 1)
    lo = lane < A_DH
    zero = jnp.zeros_like(q_h)
    q0 = jnp.where(lo, q_h, zero)
    q1 = jnp.where(lo, zero, q_h)
    return [_dot_nt(q0, k) for k in ks], [_dot_nt(q1, k) for k in ks]


def _head_output(scores, vs, lam, sg, lam_init):
    def soft(ss):
        m = functools.reduce(jnp.maximum, [jnp.max(s, axis=-1, keepdims=True) for s in ss])
        es = [jnp.exp2(s - m) for s in ss]
        den = functools.reduce(lambda a, b: a + b, [jnp.sum(e, axis=-1, keepdims=True) for e in es])
        return es, den

    e0, l0 = soft(scores[0])
    e1, l1 = soft(scores[1])
    r0 = 1.0 / l0
    r1 = lam / l1
    o = None
    for a, b, v in zip(e0, e1, vs):
        part = _dot((a * r0 - b * r1).astype(BF16), v)
        o = part if o is None else o + part
    ms = jnp.mean(o * o, axis=-1, keepdims=True)
    return (o * lax.rsqrt(ms + EPS) * sg) * (1.0 - lam_init)


def _attend(q_ref, key_refs, value_refs, lam, sg, lam_init, o_ref):
    def head(ref, h):
        return ref(slice(h * LANES, (h + 1) * LANES))

    nxt = _head_scores(head(q_ref, 0), [head(k, 0) for k in key_refs])
    for h in range(A_HEADS):
        cur = nxt
        if h + 1 < A_HEADS:
            nxt = _head_scores(head(q_ref, h + 1), [head(k, h + 1) for k in key_refs])
        o = _head_output(cur, [head(v, h) for v in value_refs], lam, sg, lam_init)
        o_ref[:, h * LANES:(h + 1) * LANES] = o.astype(BF16)


def _attn_ctx_kernel(q_ref, k_ref, v_ref, lq1, lk1, lq2, lk2, sg_ref, o_ref, *, lam_init):
    lam = _lambda(lq1, lk1, lq2, lk2, lam_init)
    _attend(lambda sl: q_ref[:, sl], [lambda sl: k_ref[:, sl]], [lambda sl: v_ref[:, sl]],
            lam, sg_ref[...], lam_init, o_ref)


def _attn_lat_kernel(q_ref, k_ref, v_ref, ck_ref, cv_ref, lq1, lk1, lq2, lk2, sg_ref, oin_ref, o_ref,
                     *, lam_init):
    del oin_ref
    lam = _lambda(lq1, lk1, lq2, lk2, lam_init)
    _attend(lambda sl: q_ref[:, sl],
            [lambda sl: k_ref[:, sl], lambda sl: ck_ref[0, 0, :, sl]],
            [lambda sl: v_ref[:, sl], lambda sl: cv_ref[0, 0, :, sl]],
            lam, sg_ref[...], lam_init, o_ref)


def _attention_call(l, proj, cache_k4, cache_v4, lams, sg, lam_init, dims):
    NC, NL, S, DS = dims
    M = NC + NL
    B = NC // S
    DB = NL // DS
    lam_specs1 = [pl.BlockSpec((1, A_DH), lambda b: (0, 0))] * 4
    oa = pl.pallas_call(
        functools.partial(_attn_ctx_kernel, lam_init=lam_init),
        grid=(B,),
        in_specs=[
            pl.BlockSpec((S, A_WIDTH), lambda b: (b, J_Q)),
            pl.BlockSpec((S, A_WIDTH), lambda b: (b, J_K)),
            pl.BlockSpec((S, A_WIDTH), lambda b: (b, J_V)),
            *lam_specs1,
            pl.BlockSpec((1, LANES), lambda b: (0, 0)),
        ],
        out_specs=pl.BlockSpec((S, A_WIDTH), lambda b: (b, 0)),
        out_shape=jax.ShapeDtypeStruct((M, A_WIDTH), BF16),
        compiler_params=_cparams(("arbitrary",), 32),
        name="attn_ctx",
    )(proj, proj, proj, *lams, sg)

    nq = DS // TQ
    q0 = NC // TQ
    k0 = NC // DS
    P = cache_k4.shape[2]
    lam_specs2 = [pl.BlockSpec((1, A_DH), lambda b, qi: (0, 0))] * 4
    oa = pl.pallas_call(
        functools.partial(_attn_lat_kernel, lam_init=lam_init),
        grid=(DB, nq),
        in_specs=[
            pl.BlockSpec((TQ, A_WIDTH), lambda b, qi: (q0 + b * nq + qi, J_Q)),
            pl.BlockSpec((DS, A_WIDTH), lambda b, qi: (k0 + b, J_K)),
            pl.BlockSpec((DS, A_WIDTH), lambda b, qi: (k0 + b, J_V)),
            pl.BlockSpec((1, 1, P, A_WIDTH), lambda b, qi: (b, l, 0, 0)),
            pl.BlockSpec((1, 1, P, A_WIDTH), lambda b, qi: (b, l, 0, 0)),
            *lam_specs2,
            pl.BlockSpec((1, LANES), lambda b, qi: (0, 0)),
            pl.BlockSpec(memory_space=pl.ANY),
        ],
        out_specs=pl.BlockSpec((TQ, A_WIDTH), lambda b, qi: (q0 + b * nq + qi, 0)),
        out_shape=jax.ShapeDtypeStruct((M, A_WIDTH), BF16),
        input_output_aliases={10: 0},
        compiler_params=_cparams(("arbitrary", "arbitrary"), 48),
        name="attn_lat",
    )(proj, proj, proj, cache_k4, cache_v4, *lams, sg, oa)
    return oa


def _mix_kernel(pc_ref, bc_ref, wp_ref, ps_ref, cw_ref, cb_ref, o_ref, *, n_ctx_tiles, S, DS):
    i = pl.program_id(0)
    tm = pc_ref.shape[0]
    nb = tm // MIX_BLK
    is_lat = i >= n_ctx_tiles
    seqlen = jnp.where(is_lat, DS, S)
    rows = lax.broadcasted_iota(I32, (tm, 1), 0)
    pos = rows & (seqlen - 1)

    t = lax.broadcasted_iota(I32, (MIX_BLK, MIX_BLK), 0)
    s = lax.broadcasted_iota(I32, (MIX_BLK, MIX_BLK), 1)
    for g, win in enumerate(P_WINDOWS):
        half = win // 2
        sl = slice(g * P_GC, (g + 1) * P_GC)
        d_cur = s - t
        band_cur = jnp.logical_and(d_cur >= -half, d_cur < half).astype(BF16)
        d_prev = d_cur - MIX_BLK
        band_prev = jnp.logical_and(is_lat, jnp.logical_and(d_prev >= -half, d_prev < half)).astype(BF16)
        d_next = d_cur + MIX_BLK
        band_next = jnp.logical_and(is_lat, jnp.logical_and(d_next >= -half, d_next < half)).astype(BF16)
        for b in range(nb):
            r0 = b * MIX_BLK
            u_b = pc_ref[r0:r0 + MIX_BLK, sl]
            acc = _dot(band_cur, u_b)
            if b > 0:
                acc = acc + _dot(band_prev, pc_ref[r0 - MIX_BLK:r0, sl])
            if b + 1 < nb:
                acc = acc + _dot(band_next, pc_ref[r0 + MIX_BLK:r0 + 2 * MIX_BLK, sl])
            p = pos[r0:r0 + MIX_BLK]
            inv_cnt = 1.0 / (jnp.minimum(p + half, seqlen) - jnp.maximum(p - half, 0)).astype(F32)
            y = (acc * inv_cnt - u_b.astype(F32)).astype(BF16)
            o_ref[r0:r0 + MIX_BLK, sl] = (_dot(y, wp_ref[g]) * ps_ref[:, sl]).astype(BF16)

    u = pc_ref[:, P_WIDTH:].astype(F32)
    gate_b = bc_ref[:, :C_WIDTH].astype(F32)
    gate_c = bc_ref[:, C_WIDTH:].astype(F32)
    z = gate_c * u
    zm = jnp.where(pos >= 1, pltpu.roll(z, 1, 0), 0.0)
    zp = jnp.where(pos + 1 < seqlen, pltpu.roll(z, tm - 1, 0), 0.0)
    conv = zm * cw_ref[0:1, :] + z * cw_ref[1:2, :] + zp * cw_ref[2:3, :] + cb_ref[...]
    o_ref[:, P_WIDTH:] = (gate_b * conv).astype(BF16)


def _mix_call(proj, w_pool_l, pool_scale_l, conv_w_l, conv_b_l, dims):
    NC, NL, S, DS = dims
    M = NC + NL
    tm = TM_MIX
    assert S == MIX_BLK and tm == DS and NC % tm == 0 and max(P_WINDOWS) // 2 <= MIX_BLK
    kern = functools.partial(_mix_kernel, n_ctx_tiles=NC // tm, S=S, DS=DS)
    return pl.pallas_call(
        kern,
        grid=(M // tm,),
        in_specs=[
            pl.BlockSpec((tm, COL_BLK), lambda i: (i, J_PC)),
            pl.BlockSpec((tm, COL_BLK), lambda i: (i, J_BC)),
            pl.BlockSpec((P_GROUPS, P_GC, P_GC), lambda i: (0, 0, 0)),
            pl.BlockSpec((1, P_WIDTH), lambda i: (0, 0)),
            pl.BlockSpec((3, C_WIDTH), lambda i: (0, 0)),
            pl.BlockSpec((1, C_WIDTH), lambda i: (0, 0)),
        ],
        out_specs=pl.BlockSpec((tm, P_WIDTH + C_WIDTH), lambda i: (i, 0)),
        out_shape=jax.ShapeDtypeStruct((M, P_WIDTH + C_WIDTH), BF16),
        compiler_params=_cparams(("arbitrary",), 48),
        name="mixers",
    )(proj, proj, w_pool_l, pool_scale_l, conv_w_l, conv_b_l)


def _route(logits):
    lane = lax.broadcasted_iota(I32, logits.shape, 1)
    lane_f = lane.astype(F32)
    neg = -jnp.inf

    def first_argmax(v, vmax):
        return jnp.min(jnp.where(v == vmax, lane_f, float(LANES)), axis=-1, keepdims=True).astype(I32)

    gl = jnp.where(lane < N_GROUPS, logits, neg)
    gmax = jnp.max(gl, axis=-1, keepdims=True)
    gsel = first_argmax(gl, gmax)
    g_w = 1.0 / jnp.sum(jnp.exp(gl - gmax), axis=-1, keepdims=True)

    first = N_GROUPS + gsel * E_PER_GROUP
    in_grp = jnp.logical_and(lane >= first, lane < first + E_PER_GROUP)
    el = jnp.where(in_grp, logits, neg)
    t1 = jnp.max(el, axis=-1, keepdims=True)
    i1 = first_argmax(el, t1)
    el2 = jnp.where(lane == i1, neg, el)
    t2 = jnp.max(el2, axis=-1, keepdims=True)
    i2 = first_argmax(el2, t2)
    a = jnp.exp(t2 - t1)
    w1 = g_w / (1.0 + a)
    w2 = g_w * a / (1.0 + a)

    swap = i2 < i1
    llo = jnp.where(swap, i2, i1) - first
    lhi = jnp.where(swap, i1, i2) - first
    w_lo = jnp.where(swap, w2, w1)
    w_hi = jnp.where(swap, w1, w2)
    pair = ((llo * (7 - llo)) >> 1) + (lhi - llo - 1)
    cls = gsel * N_PAIRS + pair
    wvec = jnp.where(lane == 0, w_lo, jnp.where(lane == 1, w_hi, 0.0))
    return cls, wvec


def _post_kernel(x_ref, oa_ref, opc_ref, g0_ref, g1_ref, g2_ref, mod_ref, n2_ref, wa_ref, wp_ref, wc_ref,
                 wo_ref, wr_ref, br_ref, x1_ref, h2w_ref, cls_ref):
    a = _dot(oa_ref[...], wa_ref[0])
    merged = g0_ref[...].astype(F32) * a
    p = _dot(opc_ref[:, :P_WIDTH], wp_ref[0])
    merged = merged + g1_ref[...].astype(F32) * p
    c = _dot(opc_ref[:, P_WIDTH:], wc_ref[0])
    merged = merged + g2_ref[...].astype(F32) * c
    y = _dot(merged.astype(BF16), wo_ref[0])
    x1 = x_ref[...] + mod_ref[0, 2:3, :] * y
    x1_ref[...] = x1
    ms = jnp.mean(x1 * x1, axis=-1, keepdims=True)
    h2 = x1 * lax.rsqrt(ms + EPS) * n2_ref[...]
    h2 = h2 * (1.0 + mod_ref[0, 4:5, :]) + mod_ref[0, 3:4, :]
    h2b = h2.astype(BF16)
    h2w_ref[:, :HALF] = _pack_halves(h2b)
    logits = _dot(h2b, wr_ref[0]) + br_ref[0]
    cls, wvec = _route(logits)
    h2w_ref[:, HALF:] = lax.bitcast_convert_type(wvec, U32)
    cls_ref[...] = jnp.broadcast_to(cls, cls_ref.shape)


def _post_call(l, x_all, oa, opc, proj, mod_l, g2, wa, wp, wc, wo, wr, br, dims):
    NC, NL, S, DS = dims
    M = NC + NL
    tm = TM_POST
    nct = NC // tm
    tpl = DS // tm

    def modrow(i):
        return jnp.where(i < nct, 0, 1 + (i - nct) // tpl)

    const3 = lambda i: (l, 0, 0)
    single = pl.Buffered(1)
    return pl.pallas_call(
        _post_kernel,
        grid=(M // tm,),
        in_specs=[
            pl.BlockSpec((tm, D), lambda i: (i, 0)),
            pl.BlockSpec((tm, A_WIDTH), lambda i: (i, 0)),
            pl.BlockSpec((tm, P_WIDTH + C_WIDTH), lambda i: (i, 0)),
            pl.BlockSpec((tm, D), lambda i: (i, 0)),
            pl.BlockSpec((tm, D), lambda i: (i, 1)),
            pl.BlockSpec((tm, D), lambda i: (i, 2)),
            pl.BlockSpec((1, N_MOD, D), lambda i: (modrow(i), 0, 0)),
            pl.BlockSpec((1, D), lambda i: (0, 0)),
            pl.BlockSpec((1, A_WIDTH, D), const3, pipeline_mode=single),
            pl.BlockSpec((1, P_WIDTH, D), const3, pipeline_mode=single),
            pl.BlockSpec((1, C_WIDTH, D), const3, pipeline_mode=single),
            pl.BlockSpec((1, D, D), const3, pipeline_mode=single),
            pl.BlockSpec((1, D, LANES), const3, pipeline_mode=single),
            pl.BlockSpec((1, 1, LANES), const3),
        ],
        out_specs=[
            pl.BlockSpec((tm, D), lambda i: (i, 0)),
            pl.BlockSpec((tm, XS_COLS), lambda i: (i, 0)),
            pl.BlockSpec((tm, LANES), lambda i: (i, 0)),
        ],
        out_shape=[
            jax.ShapeDtypeStruct((M, D), F32),
            jax.ShapeDtypeStruct((M, XS_COLS), U32),
            jax.ShapeDtypeStruct((M, LANES), I32),
        ],
        compiler_params=_cparams(("arbitrary",), 52),
        name="post_attn",
    )(x_all, oa, opc, proj, proj, proj, mod_l, g2, wa, wp, wc, wo, wr, br)


def _sort_plan(cls, n_chunks):
    tm = TM_MOE
    onehot = (cls[:, None] == jnp.arange(N_CLASSES, dtype=I32)[None, :]).astype(I32)
    csum = jnp.cumsum(onehot, axis=0)
    rank = jnp.sum(onehot * csum, axis=1) - 1
    counts = csum[-1]
    padded = ((counts + tm - 1) // tm) * tm
    ends = jnp.cumsum(padded)
    starts = ends - padded
    pos = (jnp.sum(onehot * starts[None, :], axis=1) + rank).astype(I32)
    used = ends[-1] // tm
    chunk = jnp.arange(n_chunks, dtype=I32)
    cidx = jnp.minimum(chunk, used - 1)
    ccls = jnp.sum((ends[None, :] <= (cidx * tm)[:, None]).astype(I32), axis=1)
    ea = jnp.asarray(_CLASS_EA)[ccls]
    eb = jnp.asarray(_CLASS_EB)[ccls]
    valid = (chunk < used).astype(I32)
    return pos, cidx.astype(I32), ea, eb, valid


def _dispatch_kernel(pos_ref, h2w_ref, xs_in_ref, xs_ref, sem):
    del xs_in_ref
    tm = h2w_ref.shape[0]
    base = pl.program_id(0) * tm

    for r in range(tm):
        dst = pos_ref[base + r]
        pltpu.make_async_copy(h2w_ref.at[pl.ds(r, 1)], xs_ref.at[pl.ds(dst, 1)],
                              sem).start(priority=r % N_DMA_PRIORITIES)

    def drain(r, carry):
        pltpu.make_async_copy(h2w_ref.at[pl.ds(0, 1)], xs_ref.at[pl.ds(0, 1)], sem).wait()
        return carry

    lax.fori_loop(0, tm, drain, 0, unroll=8)


def _dispatch_call(pos, h2w, xs):
    M = h2w.shape[0]
    tm = TM_DISP
    return pl.pallas_call(
        _dispatch_kernel,
        grid_spec=pltpu.PrefetchScalarGridSpec(
            num_scalar_prefetch=1,
            grid=(M // tm,),
            in_specs=[
                pl.BlockSpec((tm, XS_COLS), lambda i, pos: (i, 0)),
                pl.BlockSpec(memory_space=pl.ANY),
            ],
            out_specs=pl.BlockSpec(memory_space=pl.ANY),
            scratch_shapes=[pltpu.SemaphoreType.DMA(())],
        ),
        out_shape=jax.ShapeDtypeStruct(xs.shape, U32),
        input_output_aliases={2: 0},
        compiler_params=_cparams(("arbitrary",), 32),
        name="dispatch",
    )(pos, h2w, xs)


def _moe_kernel(cidx_ref, ea_ref, eb_ref, valid_ref, xs_ref, wga, wua, wda, wgb, wub, wdb, ys_ref):
    del cidx_ref, ea_ref, eb_ref
    c = pl.program_id(0)

    @pl.when(valid_ref[c] == 1)
    def _():
        xa, xb = _unpack_halves(xs_ref[:, :HALF])
        x = jnp.concatenate([xa.astype(BF16), xb.astype(BF16)], axis=1)
        wts = lax.bitcast_convert_type(xs_ref[:, HALF:], F32)

        def expert(wg, wu, wd, w):
            g = _dot(x, wg[0, 0])
            u = _dot(x, wu[0, 0])
            act = (g * jax.nn.sigmoid(g)) * u
            return _dot((act * w).astype(BF16), wd[0, 0])

        y = expert(wga, wua, wda, wts[:, 0:1]) + expert(wgb, wub, wdb, wts[:, 1:2])
        ys_ref[...] = _pack_halves(y.astype(BF16))


def _moe_call(l, cidx, ea, eb, valid, xs, wg, wu, wd):
    n_chunks = cidx.shape[0]
    tm = TM_MOE
    up_a = pl.BlockSpec((1, 1, D, D_EXPERT), lambda c, ci, a, b, v: (l, a[c], 0, 0))
    up_b = pl.BlockSpec((1, 1, D, D_EXPERT), lambda c, ci, a, b, v: (l, b[c], 0, 0))
    dn_a = pl.BlockSpec((1, 1, D_EXPERT, D), lambda c, ci, a, b, v: (l, a[c], 0, 0))
    dn_b = pl.BlockSpec((1, 1, D_EXPERT, D), lambda c, ci, a, b, v: (l, b[c], 0, 0))
    return pl.pallas_call(
        _moe_kernel,
        grid_spec=pltpu.PrefetchScalarGridSpec(
            num_scalar_prefetch=4,
            grid=(n_chunks,),
            in_specs=[
                pl.BlockSpec((tm, XS_COLS), lambda c, ci, a, b, v: (ci[c], 0)),
                up_a, up_a, dn_a, up_b, up_b, dn_b,
            ],
            out_specs=pl.BlockSpec((tm, HALF), lambda c, ci, a, b, v: (ci[c], 0)),
        ),
        out_shape=jax.ShapeDtypeStruct((n_chunks * tm, HALF), U32),
        compiler_params=_cparams(("arbitrary",), 48),
        name="moe_experts",
    )(cidx, ea, eb, valid, xs, wg, wu, wd, wg, wu, wd)


def _combine_kernel(pos_ref, x1_ref, mod_ref, ys_ref, *rest, n_ctx_tiles, split):
    if split:
        op_ref, os_ref, ybuf, sem = rest
    else:
        o_ref, ybuf, sem = rest
    tm = x1_ref.shape[0]
    i = pl.program_id(0)
    n = pl.num_programs(0)

    def issue(step, slot):
        base = step * tm
        for r in range(tm):
            src = pos_ref[base + r]
            pltpu.make_async_copy(ys_ref.at[pl.ds(src, 1)], ybuf.at[slot, pl.ds(r, 1)],
                                  sem.at[slot]).start(priority=r % N_DMA_PRIORITIES)

    @pl.when(i == 0)
    def _():
        issue(0, 0)

    for nxt_slot in range(2):
        @pl.when(jnp.logical_and(i + 1 < n, (i + 1) % 2 == nxt_slot))
        def _():
            issue(i + 1, nxt_slot)

    slot = i % 2

    def drain(r, carry):
        pltpu.make_async_copy(ys_ref.at[pl.ds(0, 1)], ybuf.at[slot, pl.ds(0, 1)], sem.at[slot]).wait()
        return carry

    lax.fori_loop(0, tm, drain, 0, unroll=8)
    ya, yb = _unpack_halves(ybuf[slot])
    out = x1_ref[...] + mod_ref[0, 5:6, :] * jnp.concatenate([ya, yb], axis=1)
    if split:
        @pl.when(i < n_ctx_tiles)
        def _():
            op_ref[...] = out

        @pl.when(i >= n_ctx_tiles)
        def _():
            os_ref[...] = out
    else:
        o_ref[...] = out


def _combine_call(pos, x1, mod_l, ys, dims, split):
    NC, NL, S, DS = dims
    M = NC + NL
    tm = TM_COMB
    nct = NC // tm
    tpl = DS // tm

    def modrow(i, pos):
        return (jnp.where(i < nct, 0, 1 + (i - nct) // tpl), 0, 0)

    if split:
        out_specs = [pl.BlockSpec((tm, D), lambda i, pos: (jnp.minimum(i, nct - 1), 0)),
                     pl.BlockSpec((tm, D), lambda i, pos: (jnp.maximum(i - nct, 0), 0))]
        out_shape = [jax.ShapeDtypeStruct((NC, D), F32), jax.ShapeDtypeStruct((NL, D), F32)]
    else:
        out_specs = pl.BlockSpec((tm, D), lambda i, pos: (i, 0))
        out_shape = jax.ShapeDtypeStruct((M, D), F32)
    return pl.pallas_call(
        functools.partial(_combine_kernel, n_ctx_tiles=nct, split=split),
        grid_spec=pltpu.PrefetchScalarGridSpec(
            num_scalar_prefetch=1,
            grid=(M // tm,),
            in_specs=[
                pl.BlockSpec((tm, D), lambda i, pos: (i, 0)),
                pl.BlockSpec((1, N_MOD, D), modrow),
                pl.BlockSpec(memory_space=pl.ANY),
            ],
            out_specs=out_specs,
            scratch_shapes=[pltpu.VMEM((2, tm, HALF), U32), pltpu.SemaphoreType.DMA((2,))],
        ),
        out_shape=out_shape,
        compiler_params=_cparams(("arbitrary",), 44),
        name="combine",
    )(pos, x1, mod_l, ys)


def _rope_tables(DS, tm):
    rows = DS // GRID_W
    pos_r = jnp.repeat(jnp.arange(rows, dtype=F32), GRID_W)
    pos_c = jnp.tile(jnp.arange(GRID_W, dtype=F32), rows)
    inv_freq = jnp.power(ROPE_THETA, -jnp.arange(ROPE_FREQ, dtype=F32) / ROPE_FREQ)
    ang = jnp.stack([pos_r[:, None] * inv_freq, pos_c[:, None] * inv_freq], axis=1)
    cos, sin = jnp.cos(ang), jnp.sin(ang)
    zeros = jnp.zeros_like(sin)
    c = jnp.tile(jnp.stack([cos, cos], axis=2).reshape(DS, A_DH), (1, 2))
    sm = jnp.tile(jnp.stack([-sin, zeros], axis=2).reshape(DS, A_DH), (1, 2))
    sp = jnp.tile(jnp.stack([zeros, sin], axis=2).reshape(DS, A_DH), (1, 2))
    ident = jnp.ones((tm, LANES), F32)
    zpad = jnp.zeros((tm, LANES), F32)
    return (jnp.concatenate([ident, c], 0), jnp.concatenate([zpad, sm], 0), jnp.concatenate([zpad, sp], 0))


def _qk_tables(rope, q_gain, k_gain):
    c, sm, sp = rope
    partner = np.arange(LANES) ^ ROPE_FREQ
    out = []
    for gain, scale in ((q_gain, A_DH ** -0.5 * math.log2(math.e)), (k_gain, 1.0)):
        g = jnp.tile(gain, 2) * scale
        out += [c * g, (sm + sp) * g[partner]]
    return jnp.stack(out, axis=0)


def kernel(x_prompt, x_sample, cache_k, cache_v, c, c_ctx, w_ada, b_ada, norm1_g, norm2_g, w_in, q_norm_g,
           k_norm_g, lam_q1, lam_k1, lam_q2, lam_k2, subln_g, w_pool, pool_scale, conv_w, conv_b, w_br_a,
           w_br_p, w_br_c, w_out, w_route_group, b_route_group, w_route_expert, b_route_expert, w_exp_gate,
           w_exp_up, w_exp_down):
    B, S, _ = x_prompt.shape
    DB, DS, _ = x_sample.shape
    L = w_in.shape[0]
    P = cache_k.shape[2]
    NC, NL = B * S, DB * DS
    M = NC + NL
    dims = (NC, NL, S, DS)
    assert DB + 1 <= 8 and NC % DS == 0 and NC % TM_IN == 0 and DS % TM_IN == 0 and TM_IN % S == 0
    assert S & (S - 1) == 0 and DS & (DS - 1) == 0

    cond8 = jnp.concatenate([c_ctx[None], c, jnp.zeros((8 - 1 - DB, D), F32)], axis=0)
    mod = _ada_call(cond8, w_ada, b_ada[:, None, :]).reshape(L, 8, N_MOD, D)

    w_in_b = w_in.astype(BF16)
    wa, wp, wc, wo = (w.astype(BF16) for w in (w_br_a, w_br_p, w_br_c, w_out))
    wg, wu, wd = (w.astype(BF16) for w in (w_exp_gate, w_exp_up, w_exp_down))
    wpool = w_pool.astype(BF16)
    wr = jnp.concatenate([w_route_group, w_route_expert,
                          jnp.zeros((L, D, LANES - N_GROUPS - N_EXPERTS), F32)], axis=2).astype(BF16)
    br = jnp.concatenate([b_route_group, b_route_expert,
                          jnp.zeros((L, LANES - N_GROUPS - N_EXPERTS), F32)], axis=1)[:, None, :]

    rope = _rope_tables(DS, TM_IN)
    cache_k4 = cache_k.astype(BF16).reshape(DB, L, P, A_WIDTH)
    cache_v4 = cache_v.astype(BF16).reshape(DB, L, P, A_WIDTH)

    x_all = jnp.concatenate([x_prompt.reshape(NC, D), x_sample.reshape(NL, D)], axis=0)
    kc = vc = None
    n_chunks = (M + N_CLASSES * (TM_MOE - 1)) // TM_MOE + 1
    xs = jnp.zeros((n_chunks * TM_MOE, XS_COLS), U32)

    for l in range(L):
        lam_init = 0.8 - 0.6 * math.exp(-0.3 * l)
        mod_l = mod[l]
        tabs = _qk_tables(rope, q_norm_g[l], k_norm_g[l])
        proj, kc, vc = _inproj_call(l, x_all, mod_l, norm1_g[l][None], w_in_b, tabs, kc, vc,
                                    (B, L, S, A_WIDTH), dims)
        lams = [a[l][None] for a in (lam_q1, lam_k1, lam_q2, lam_k2)]
        oa = _attention_call(l, proj, cache_k4, cache_v4, lams, subln_g[l][None], lam_init, dims)
        opc = _mix_call(proj, wpool[l], pool_scale[l][None], conv_w[l], conv_b[l][None], dims)
        x1, h2w, cls = _post_call(l, x_all, oa, opc, proj, mod_l, norm2_g[l][None],
                                  wa, wp, wc, wo, wr, br, dims)
        pos, cidx, ea, eb, valid = _sort_plan(cls[:, 0], n_chunks)
        xs = _dispatch_call(pos, h2w, xs)
        ys = _moe_call(l, cidx, ea, eb, valid, xs, wg, wu, wd)
        x_all = _combine_call(pos, x1, mod_l, ys, dims, split=(l == L - 1))

    y_p, y_s = x_all
    return (y_p.reshape(B, S, D), y_s.reshape(DB, DS, D),
            kc.reshape(B, L, S, A_HEADS, 2, A_DH), vc.reshape(B, L, S, A_HEADS, A_DV))
```
